```python
import jax, jax.numpy as jnp
from jax import lax
import numpy as np

D_MODEL = 1024
BATCH = 8
SEQ = 2048
DEPTH = 4

N_MEM = 256
EPS = 1e-6
D_A = D_MODEL
CONV_A_WIDTH = 3
FOX_HEADS = 16
FOX_HEAD_DIM = D_MODEL // FOX_HEADS
D_B = FOX_HEADS * FOX_HEAD_DIM
Q_BLOCK = 128
D_C = D_MODEL
CONV_C_WIDTH = 31
N_BRANCH = 3
X_HEADS = 4
X_HEAD_DIM = 128
D_X = X_HEADS * X_HEAD_DIM
D_FF = -(-8 * D_MODEL // (3 * 256)) * 256
IN_SIZES = (D_A, D_A, D_A, D_B, D_B, D_B, FOX_HEADS, D_C, D_C, D_MODEL, D_MODEL, D_MODEL)
D_IN_TOTAL = sum(IN_SIZES)

kernel_name = "hybrid_gated_conv_fox_conformer_decoder"


def rms_norm(x, g):
    xf = x.astype(jnp.float32)
    y = xf * lax.rsqrt(jnp.mean(xf * xf, axis=-1, keepdims=True) + EPS)
    return (y * g.astype(jnp.float32)).astype(x.dtype)


def layer_norm(x, g, b):
    xf = x.astype(jnp.float32)
    mu = jnp.mean(xf, axis=-1, keepdims=True)
    xc = xf - mu
    y = xc * lax.rsqrt(jnp.mean(xc * xc, axis=-1, keepdims=True) + EPS)
    return (y * g.astype(jnp.float32) + b.astype(jnp.float32)).astype(x.dtype)


def causal_depthwise_conv(x, w):
    K, C = w.shape
    return lax.conv_general_dilated(
        x, w[:, None, :].astype(x.dtype), window_strides=(1,), padding=[(K - 1, 0)],
        dimension_numbers=("NWC", "WIO", "NWC"), feature_group_count=C)


def split_columns(z):
    offs = np.cumsum(np.array(IN_SIZES))[:-1]
    return jnp.split(z, [int(o) for o in offs], axis=-1)


def forgetting_attention(q, k, v, log_f):
    B, T, H, Dh = q.shape
    c = jnp.cumsum(log_f, axis=1).transpose(0, 2, 1)
    scale = Dh ** -0.5
    outs = []
    for i in range(T // Q_BLOCK):
        q0 = i * Q_BLOCK
        k_end = q0 + Q_BLOCK
        s = jnp.einsum("bqhd,bkhd->bhqk", q[:, q0:k_end], k[:, :k_end]).astype(jnp.float32) * scale
        decay = c[:, :, q0:k_end, None] - c[:, :, None, :k_end]
        causal = (q0 + jnp.arange(Q_BLOCK))[:, None] >= jnp.arange(k_end)[None, :]
        s = jnp.where(causal, s + decay, -jnp.inf)
        p = jax.nn.softmax(s, axis=-1)
        outs.append(jnp.einsum("bhqk,bkhd->bqhd", p.astype(v.dtype), v[:, :k_end]))
    return jnp.concatenate(outs, axis=1)


def hybrid_mixer(h, w_in, b_gate, b_forget, conv_a, w_out_a, w_out_b, conv_c, conv_c_bias,
                 ln_c_gain, ln_c_bias, w_out_c, w_o):
    B, T, _ = h.shape
    z = jnp.einsum("btd,de->bte", h, w_in)
    (a_b, a_c, a_u, q, k, v, f_logit, c_val, c_gate, g_a, g_b, g_c) = split_columns(z)
    y_a = a_b * causal_depthwise_conv(a_c * a_u, conv_a)
    y_a = jnp.einsum("btc,cd->btd", y_a, w_out_a)
    log_f = jax.nn.log_sigmoid(f_logit.astype(jnp.float32) + b_forget.astype(jnp.float32))
    o = forgetting_attention(q.reshape(B, T, FOX_HEADS, FOX_HEAD_DIM),
                             k.reshape(B, T, FOX_HEADS, FOX_HEAD_DIM),
                             v.reshape(B, T, FOX_HEADS, FOX_HEAD_DIM), log_f)
    y_b = jnp.einsum("btc,cd->btd", o.reshape(B, T, D_B), w_out_b)
    u = c_val * jax.nn.sigmoid(c_gate)
    u = causal_depthwise_conv(u, conv_c) + conv_c_bias.astype(u.dtype)
    u = jax.nn.silu(layer_norm(u, ln_c_gain, ln_c_bias))
    y_c = jnp.einsum("btc,cd->btd", u, w_out_c)
    ga, gb, gc = jnp.split(jax.nn.sigmoid(jnp.concatenate([g_a, g_b, g_c], axis=-1) + b_gate), N_BRANCH, axis=-1)
    merged = ga * y_a + gb * y_b + gc * y_c
    return jnp.einsum("btd,de->bte", merged, w_o)


def memory_cross_attention(h, mem_n, w_xq, w_xkv, w_xo):
    B, T, _ = h.shape
    q = jnp.einsum("btd,de->bte", h, w_xq).reshape(B, T, X_HEADS, X_HEAD_DIM)
    kv = jnp.einsum("bmd,de->bme", mem_n, w_xkv)
    k, v = jnp.split(kv, 2, axis=-1)
    k = k.reshape(B, -1, X_HEADS, X_HEAD_DIM)
    v = v.reshape(B, -1, X_HEADS, X_HEAD_DIM)
    s = jnp.einsum("bthd,bmhd->bhtm", q, k).astype(jnp.float32) * (X_HEAD_DIM ** -0.5)
    p = jax.nn.softmax(s, axis=-1)
    o = jnp.einsum("bhtm,bmhd->bthd", p.astype(v.dtype), v).reshape(B, T, D_X)
    return jnp.einsum("btc,cd->btd", o, w_xo)


def swiglu_ffn(h, w_gate_up, w_down):
    g, u = jnp.split(jnp.einsum("btd,df->btf", h, w_gate_up), 2, axis=-1)
    return jnp.einsum("btf,fd->btd", jax.nn.silu(g) * u, w_down)


def setup_inputs(seed: int = 0) -> dict:
    key = jax.random.key(seed)
    ks = jax.random.split(key, 24)
    f32 = jnp.float32

    def w(k, shape, fan_in):
        return jax.random.normal(k, shape, f32) * (fan_in ** -0.5)

    def gain(k, shape):
        return 1.0 + 0.02 * jax.random.normal(k, shape, f32)

    def small(k, shape, s=0.02):
        return s * jax.random.normal(k, shape, f32)

    L, D = DEPTH, D_MODEL
    return {
        "x": jax.random.normal(ks[0], (BATCH, SEQ, D), f32),
        "mem": jax.random.normal(ks[1], (BATCH, N_MEM, D), f32),
        "mix_norm": gain(ks[2], (L, D)),
        "w_in": w(ks[3], (L, D, D_IN_TOTAL), D),
        "b_gate": small(ks[4], (L, N_BRANCH * D)),
        "b_forget": 2.0 + 0.5 * jax.random.normal(ks[5], (L, FOX_HEADS), f32),
        "conv_a": w(ks[6], (L, CONV_A_WIDTH, D_A), CONV_A_WIDTH),
        "w_out_a": w(ks[7], (L, D_A, D), D_A),
        "w_out_b": w(ks[8], (L, D_B, D), D_B),
        "conv_c": w(ks[9], (L, CONV_C_WIDTH, D_C), CONV_C_WIDTH),
        "conv_c_bias": small(ks[10], (L, D_C)),
        "ln_c_gain": gain(ks[11], (L, D_C)),
        "ln_c_bias": small(ks[12], (L, D_C)),
        "w_out_c": w(ks[13], (L, D_C, D), D_C),
        "w_o": w(ks[14], (L, D, D), D),
        "xattn_norm": gain(ks[15], (L, D)),
        "mem_norm": gain(ks[16], (D,)),
        "w_xq": w(ks[17], (L, D, D_X), D),
        "w_xkv": w(ks[18], (L, D, 2 * D_X), D),
        "w_xo": w(ks[19], (L, D_X, D), D_X),
        "ffn_norm": gain(ks[20], (L, D)),
        "w_gate_up": w(ks[21], (L, D, 2 * D_FF), D),
        "w_down": w(ks[22], (L, D_FF, D), D_FF),
        "final_norm": gain(ks[23], (D,)),
    }


def reference(x, mem, mix_norm, w_in, b_gate, b_forget, conv_a, w_out_a, w_out_b, conv_c,
              conv_c_bias, ln_c_gain, ln_c_bias, w_out_c, w_o, xattn_norm, mem_norm, w_xq,
              w_xkv, w_xo, ffn_norm, w_gate_up, w_down, final_norm):
    mem_n = rms_norm(mem, mem_norm)
    for l in range(DEPTH):
        x = x + hybrid_mixer(rms_norm(x, mix_norm[l]), w_in[l], b_gate[l], b_forget[l], conv_a[l],
                             w_out_a[l], w_out_b[l], conv_c[l], conv_c_bias[l], ln_c_gain[l],
                             ln_c_bias[l], w_out_c[l], w_o[l])
        x = x + memory_cross_attention(rms_norm(x, xattn_norm[l]), mem_n, w_xq[l], w_xkv[l], w_xo[l])
        x = x + swiglu_ffn(rms_norm(x, ffn_norm[l]), w_gate_up[l], w_down[l])
    return rms_norm(x, final_norm)
```

```python
import functools

import jax
import jax.numpy as jnp
from jax import lax
from jax.experimental import pallas as pl
from jax.experimental.pallas import tpu as pltpu

EPS = 1e-6
FOX_HEADS = 16
FOX_HEAD_DIM = 64
X_HEADS = 4
X_HEAD_DIM = 128
CONV_A_WIDTH = 3
CONV_C_WIDTH = 31
LANES = 128
HALO = 32
CUM_BLOCK = 256
VMEM_LIMIT = 56 * 1024 * 1024

F32 = jnp.float32
BF16 = jnp.bfloat16


def _params(*sem):
    return pltpu.CompilerParams(dimension_semantics=sem, vmem_limit_bytes=VMEM_LIMIT)


def _rms_norm(x, g):
    return x * lax.rsqrt(jnp.mean(x * x, axis=-1, keepdims=True) + EPS) * g


def _sigmoid(x):
    return 1.0 / (1.0 + jnp.exp(-x))


def _dot(a, b):
    return jnp.dot(a, b, preferred_element_type=F32)


def _dot_nt(a, b):
    return lax.dot_general(a, b, (((1,), (1,)), ((), ())), preferred_element_type=F32)


def _in_proj_kernel(x_ref, g_ref, w_ref, wf_ref, z_ref, f_ref, xn_ref):
    @pl.when(pl.program_id(1) == 0)
    def _():
        xn = _rms_norm(x_ref[...], g_ref[...]).astype(BF16)
        xn_ref[...] = xn
        f_ref[...] = _dot(xn, wf_ref[...])

    z_ref[...] = _dot(xn_ref[...], w_ref[...]).astype(z_ref.dtype)


def _in_proj(x, gain, w_main, w_f, layer, *, tm, tn):
    n, d = x.shape
    e = w_main.shape[-1]
    return pl.pallas_call(
        _in_proj_kernel,
        grid=(n // tm, e // tn),
        in_specs=[
            pl.BlockSpec((tm, d), lambda i, j: (i, 0)),
            pl.BlockSpec((None, 1, d), lambda i, j: (layer, 0, 0)),
            pl.BlockSpec((None, d, tn), lambda i, j: (layer, 0, j)),
            pl.BlockSpec((None, d, LANES), lambda i, j: (layer, 0, 0)),
        ],
        out_specs=[
            pl.BlockSpec((tm, tn), lambda i, j: (i, j)),
            pl.BlockSpec((tm, LANES), lambda i, j: (i, 0)),
        ],
        out_shape=[
            jax.ShapeDtypeStruct((n, e), BF16),
            jax.ShapeDtypeStruct((n, LANES), F32),
        ],
        scratch_shapes=[pltpu.VMEM((tm, d), BF16)],
        compiler_params=_params("parallel", "arbitrary"),
        name="in_proj",
    )(x, gain, w_main, w_f)


def _forget_cumsum_kernel(f_ref, b_ref, tri_ref, crow_ref, c_scr):
    x = f_ref[...] + b_ref[...]
    lf = jnp.minimum(x, 0.0) - jnp.log(1.0 + jnp.exp(-jnp.abs(x)))
    hi = lf.astype(BF16)
    rem = lf - hi.astype(F32)
    mid = rem.astype(BF16)
    lo = (rem - mid.astype(F32)).astype(BF16)
    tri = tri_ref[...]
    seq = f_ref.shape[0]
    carry = jnp.zeros((1, LANES), F32)
    for blk in range(seq // CUM_BLOCK):
        sl = slice(blk * CUM_BLOCK, (blk + 1) * CUM_BLOCK)
        part = _dot(tri, hi[sl]) + _dot(tri, mid[sl]) + _dot(tri, lo[sl]) + carry
        c_scr[sl, :] = part
        carry = part[CUM_BLOCK - 1:CUM_BLOCK, :]
    crow_ref[...] = c_scr[...].T[:FOX_HEADS, :]


def _forget_cumsum(f, b_forget, layer, tri, *, batch, seq):
    return pl.pallas_call(
        _forget_cumsum_kernel,
        grid=(batch,),
        in_specs=[
            pl.BlockSpec((seq, LANES), lambda b: (b, 0)),
            pl.BlockSpec((None, 1, LANES), lambda b: (layer, 0, 0)),
            pl.BlockSpec((CUM_BLOCK, CUM_BLOCK), lambda b: (0, 0)),
        ],
        out_specs=pl.BlockSpec((None, FOX_HEADS, seq), lambda b: (b, 0, 0)),
        out_shape=jax.ShapeDtypeStruct((batch, FOX_HEADS, seq), F32),
        scratch_shapes=[pltpu.VMEM((seq, LANES), F32)],
        compiler_params=_params("parallel"),
        name="forget_cumsum",
    )(f, b_forget, tri)


def _fox_kernel(q_ref, k_ref, v_ref, crow_ref, o_ref, *, tq):
    qi = pl.program_id(2)
    lane = lax.broadcasted_iota(jnp.int32, (1, LANES), 1)
    row = lax.broadcasted_iota(jnp.int32, (tq, tq), 0)
    col = lax.broadcasted_iota(jnp.int32, (tq, tq), 1)
    q = q_ref[...]
    scale = FOX_HEAD_DIM ** -0.5
    outs = []
    for h in range(2):
        in_head = (lane >= FOX_HEAD_DIM * h) & (lane < FOX_HEAD_DIM * (h + 1))
        qh = (jnp.where(in_head, q.astype(F32), 0.0) * scale).astype(BF16)

        def step(kb, carry, diagonal):
            m, l, acc = carry
            k0 = pl.multiple_of(kb * tq, tq)
            s = _dot_nt(qh, k_ref[pl.ds(k0, tq), :]) - crow_ref[h:h + 1, pl.ds(k0, tq)]
            if diagonal:
                s = jnp.where(row >= col, s, -jnp.inf)
            m_new = jnp.maximum(m, jnp.max(s, axis=-1, keepdims=True))
            alpha = jnp.exp(m - m_new)
            p = jnp.exp(s - m_new)
            l = alpha * l + jnp.sum(p, axis=-1, keepdims=True)
            acc = alpha * acc + _dot(p.astype(BF16), v_ref[pl.ds(k0, tq), :])
            return m_new, l, acc

        init = (jnp.full((tq, 1), -jnp.inf, F32), jnp.zeros((tq, 1), F32), jnp.zeros((tq, LANES), F32))
        carry = lax.fori_loop(0, qi, functools.partial(step, diagonal=False), init)
        _, l, acc = step(qi, carry, diagonal=True)
        outs.append(acc / l)
    o_ref[...] = jnp.where(lane < FOX_HEAD_DIM, outs[0], outs[1]).astype(o_ref.dtype)


def _fox_attention(z, c_row, *, batch, seq, q_col, k_col, v_col, tq):
    n = z.shape[0]
    pairs = FOX_HEADS // 2
    nq = seq // tq
    qb, kb, vb = q_col // LANES, k_col // LANES, v_col // LANES
    return pl.pallas_call(
        functools.partial(_fox_kernel, tq=tq),
        grid=(batch, pairs, nq),
        in_specs=[
            pl.BlockSpec((tq, LANES), lambda b, p, i: (b * nq + i, qb + p)),
            pl.BlockSpec((seq, LANES), lambda b, p, i: (b, kb + p)),
            pl.BlockSpec((seq, LANES), lambda b, p, i: (b, vb + p)),
            pl.BlockSpec((None, None, 2, seq), lambda b, p, i: (b, p, 0, 0)),
        ],
        out_specs=pl.BlockSpec((tq, LANES), lambda b, p, i: (b * nq + i, p)),
        out_shape=jax.ShapeDtypeStruct((n, pairs * LANES), BF16),
        compiler_params=_params("parallel", "parallel", "arbitrary"),
        name="fox_attention",
    )(z, z, z, c_row)


def _mixer_kernel(x_ref, ab_ref, ac_ref, au_ref, cv_ref, cg_ref, ga_ref, gb_ref, gc_ref,
                  hac_ref, hau_ref, hcv_ref, hcg_ref, o_ref,
                  conva_ref, convc_ref, cbias_ref, lng_ref, lnb_ref, bgate_ref,
                  woa_ref, wob_ref, woc_ref, wo_ref, out_ref,
                  u_scr, p_scr, uc_scr, pa_scr, *, tm, seq, rows):
    d = x_ref.shape[-1]
    keep = jnp.where((pl.program_id(0) * tm) % seq == 0, 0.0, 1.0)
    u_scr[0:HALO, :] = hcv_ref[...].astype(F32) * _sigmoid(hcg_ref[...].astype(F32)) * keep
    u_scr[HALO:, :] = cv_ref[...].astype(F32) * _sigmoid(cg_ref[...].astype(F32))
    p_scr[0:HALO, :] = hac_ref[...].astype(F32) * hau_ref[...].astype(F32) * keep
    p_scr[HALO:, :] = ac_ref[...].astype(F32) * au_ref[...].astype(F32)

    for c in range(tm // rows):
        r0 = c * rows
        acc = jnp.broadcast_to(cbias_ref[...], (rows, d))
        base = r0 + HALO - (CONV_C_WIDTH - 1)
        for k in range(CONV_C_WIDTH):
            acc = acc + convc_ref[k:k + 1, :] * u_scr[base + k:base + k + rows, :]
        mu = jnp.mean(acc, axis=-1, keepdims=True)
        xc = acc - mu
        y = xc * lax.rsqrt(jnp.mean(xc * xc, axis=-1, keepdims=True) + EPS) * lng_ref[...] + lnb_ref[...]
        uc_scr[r0:r0 + rows, :] = (y * _sigmoid(y)).astype(BF16)
        base = r0 + HALO - (CONV_A_WIDTH - 1)
        acc = conva_ref[0:1, :] * p_scr[base:base + rows, :]
        for k in range(1, CONV_A_WIDTH):
            acc = acc + conva_ref[k:k + 1, :] * p_scr[base + k:base + k + rows, :]
        pa_scr[r0:r0 + rows, :] = (ab_ref[r0:r0 + rows, :].astype(F32) * acc).astype(BF16)

    bg = bgate_ref[...]
    merged = _sigmoid(ga_ref[...].astype(F32) + bg[0:1, :]) * _dot(pa_scr[...], woa_ref[...])
    merged = merged + _sigmoid(gb_ref[...].astype(F32) + bg[1:2, :]) * _dot(o_ref[...], wob_ref[...])
    merged = merged + _sigmoid(gc_ref[...].astype(F32) + bg[2:3, :]) * _dot(uc_scr[...], woc_ref[...])
    out_ref[...] = x_ref[...] + _dot(merged.astype(BF16), wo_ref[...])


def _mixer(x, z, o, conv_a, conv_c, conv_c_bias, ln_g, ln_b, b_gate, w_out_a, w_out_b, w_out_c, w_o,
           layer, cols, *, seq, tm, rows=32):
    n, d = x.shape
    hb = tm // HALO
    cb = {name: c // d for name, c in cols.items()}

    def zspec(name):
        j = cb[name]
        return pl.BlockSpec((tm, d), lambda i: (i, j))

    def hspec(name):
        j = cb[name]
        return pl.BlockSpec((HALO, d), lambda i: (jnp.maximum(i * hb - 1, 0), j))

    def vec(k):
        return pl.BlockSpec((None, k, d), lambda i: (layer, 0, 0))

    wspec = pl.BlockSpec((None, d, d), lambda i: (layer, 0, 0))
    tile = pl.BlockSpec((tm, d), lambda i: (i, 0))
    return pl.pallas_call(
        functools.partial(_mixer_kernel, tm=tm, seq=seq, rows=rows),
        grid=(n // tm,),
        in_specs=[tile] + [zspec(k) for k in ("a_b", "a_c", "a_u", "c_val", "c_gate", "g_a", "g_b", "g_c")]
        + [hspec(k) for k in ("a_c", "a_u", "c_val", "c_gate")] + [tile]
        + [vec(CONV_A_WIDTH), vec(CONV_C_WIDTH), vec(1), vec(1), vec(1), vec(3)]
        + [wspec] * 4,
        out_specs=tile,
        out_shape=jax.ShapeDtypeStruct((n, d), F32),
        scratch_shapes=[
            pltpu.VMEM((tm + HALO, d), F32),
            pltpu.VMEM((tm + HALO, d), F32),
            pltpu.VMEM((tm, d), BF16),
            pltpu.VMEM((tm, d), BF16),
        ],
        compiler_params=_params("parallel"),
        name="mixer_merge",
    )(x, *([z] * 12), o, conv_a, conv_c, conv_c_bias, ln_g, ln_b, b_gate, w_out_a, w_out_b, w_out_c, w_o)


def _mem_kv_kernel(mem_ref, g_ref, w_ref, kv_ref):
    kv_ref[...] = _dot(_rms_norm(mem_ref[...], g_ref[...]).astype(BF16), w_ref[...]).astype(kv_ref.dtype)


def _mem_kv(mem, gain, w_xkv):
    m, d = mem.shape
    depth, _, e = w_xkv.shape
    return pl.pallas_call(
        _mem_kv_kernel,
        grid=(depth,),
        in_specs=[
            pl.BlockSpec((m, d), lambda l: (0, 0)),
            pl.BlockSpec((1, d), lambda l: (0, 0)),
            pl.BlockSpec((None, d, e), lambda l: (l, 0, 0)),
        ],
        out_specs=pl.BlockSpec((None, m, e), lambda l: (l, 0, 0)),
        out_shape=jax.ShapeDtypeStruct((depth, m, e), BF16),
        compiler_params=_params("parallel"),
        name="mem_kv",
    )(mem, gain, w_xkv)


def _xattn_kernel(x_ref, g_ref, wq_ref, kv_ref, wo_ref, out_ref):
    x = x_ref[...]
    q = _dot(_rms_norm(x, g_ref[...]).astype(BF16), wq_ref[...]) * (X_HEAD_DIM ** -0.5)
    q = q.astype(BF16)
    dx = X_HEADS * X_HEAD_DIM
    outs = []
    for h in range(X_HEADS):
        sl = slice(h * X_HEAD_DIM, (h + 1) * X_HEAD_DIM)
        s = _dot_nt(q[:, sl], kv_ref[:, sl])
        p = jnp.exp(s - jnp.max(s, axis=-1, keepdims=True))
        oh = _dot(p.astype(BF16), kv_ref[:, dx + h * X_HEAD_DIM:dx + (h + 1) * X_HEAD_DIM])
        outs.append((oh / jnp.sum(p, axis=-1, keepdims=True)).astype(BF16))
    out_ref[...] = x + _dot(jnp.concatenate(outs, axis=-1), wo_ref[...])


def _xattn(x, gain, w_xq, kv, w_xo, layer, *, seq, n_mem, tm):
    n, d = x.shape
    dx = w_xq.shape[-1]
    per_seq = seq // tm
    tile = pl.BlockSpec((tm, d), lambda i: (i, 0))
    return pl.pallas_call(
        _xattn_kernel,
        grid=(n // tm,),
        in_specs=[
            tile,
            pl.BlockSpec((None, 1, d), lambda i: (layer, 0, 0)),
            pl.BlockSpec((None, d, dx), lambda i: (layer, 0, 0)),
            pl.BlockSpec((None, None, n_mem, 2 * dx), lambda i: (layer, i // per_seq, 0, 0)),
            pl.BlockSpec((None, dx, d), lambda i: (layer, 0, 0)),
        ],
        out_specs=tile,
        out_shape=jax.ShapeDtypeStruct((n, d), F32),
        compiler_params=_params("parallel"),
        name="xattn",
    )(x, gain, w_xq, kv, w_xo)


def _ffn_kernel(x_ref, g_ref, wg_ref, wu_ref, wd_ref, fg_ref, out_ref, xn_ref, *, final):
    j = pl.program_id(1)

    @pl.when(j == 0)
    def _():
        xn_ref[...] = _rms_norm(x_ref[...], g_ref[...]).astype(BF16)

    xn = xn_ref[...]
    g = _dot(xn, wg_ref[...])
    h = (g * _sigmoid(g) * _dot(xn, wu_ref[...])).astype(BF16)
    y = _dot(h, wd_ref[...])

    @pl.when(j == 0)
    def _():
        out_ref[...] = x_ref[...] + y

    @pl.when(j > 0)
    def _():
        out_ref[...] += y

    if final:
        @pl.when(j == pl.num_programs(1) - 1)
        def _():
            out_ref[...] = _rms_norm(out_ref[...], fg_ref[...])


def _ffn(x, gain, w_gate_up, w_down, final_gain, layer, *, tm, fc, final):
    n, d = x.shape
    dff = w_down.shape[1]
    nj = dff // fc
    tile = pl.BlockSpec((tm, d), lambda i, j: (i, 0))
    return pl.pallas_call(
        functools.partial(_ffn_kernel, final=final),
        grid=(n // tm, nj),
        in_specs=[
            tile,
            pl.BlockSpec((None, 1, d), lambda i, j: (layer, 0, 0)),
            pl.BlockSpec((None, d, fc), lambda i, j: (layer, 0, j)),
            pl.BlockSpec((None, d, fc), lambda i, j: (layer, 0, nj + j)),
            pl.BlockSpec((None, fc, d), lambda i, j: (layer, j, 0)),
            pl.BlockSpec((1, d), lambda i, j: (0, 0)),
        ],
        out_specs=tile,
        out_shape=jax.ShapeDtypeStruct((n, d), F32),
        scratch_shapes=[pltpu.VMEM((tm, d), BF16)],
        compiler_params=_params("parallel", "arbitrary"),
        name="ffn",
    )(x, gain, w_gate_up, w_gate_up, w_down, final_gain)


def _pick(n, pref):
    t = min(n, pref)
    assert n % t == 0, (n, t)
    return t


def kernel(x, mem, mix_norm, w_in, b_gate, b_forget, conv_a, w_out_a, w_out_b, conv_c, conv_c_bias,
           ln_c_gain, ln_c_bias, w_out_c, w_o, xattn_norm, mem_norm, w_xq, w_xkv, w_xo, ffn_norm,
           w_gate_up, w_down, final_norm):
    batch, seq, d = x.shape
    depth = w_in.shape[0]
    n_mem = mem.shape[1]
    n = batch * seq
    d_b = FOX_HEADS * FOX_HEAD_DIM
    assert d % LANES == 0 and seq % CUM_BLOCK == 0 and d_b == d

    names = ("a_b", "a_c", "a_u", "q", "k", "v", "c_val", "c_gate", "g_a", "g_b", "g_c")
    cols = {name: idx * d for idx, name in enumerate(names)}
    f_col = 6 * d
    w_main = jnp.concatenate([w_in[:, :, :f_col], w_in[:, :, f_col + FOX_HEADS:]], axis=-1).astype(BF16)
    w_f = jnp.pad(w_in[:, :, f_col:f_col + FOX_HEADS], ((0, 0), (0, 0), (0, LANES - FOX_HEADS))).astype(BF16)
    b_f = jnp.pad(b_forget, ((0, 0), (0, LANES - FOX_HEADS)))[:, None, :]
    tri = jnp.tril(jnp.ones((CUM_BLOCK, CUM_BLOCK), BF16))

    row3 = lambda a: a[:, None, :]
    bf = lambda a: a.astype(BF16)
    b_gate3 = b_gate.reshape(depth, 3, d)

    kv = _mem_kv(mem.reshape(batch * n_mem, d), mem_norm[None, :], bf(w_xkv))
    kv = kv.reshape(depth, batch, n_mem, kv.shape[-1])
    w_oa, w_ob, w_oc, w_oo = bf(w_out_a), bf(w_out_b), bf(w_out_c), bf(w_o)
    w_q, w_xo_b, w_gu, w_dn = bf(w_xq), bf(w_xo), bf(w_gate_up), bf(w_down)

    dff = w_down.shape[1]
    fc = dff // 2 if (dff // 2) % LANES == 0 else dff
    xs = x.reshape(n, d)
    for l in range(depth):
        z, f = _in_proj(xs, row3(mix_norm), w_main, w_f, l, tm=_pick(n, 1024), tn=_pick(w_main.shape[-1], 1024))
        c_row = _forget_cumsum(f, b_f, l, tri, batch=batch, seq=seq)
        c_row = c_row.reshape(batch, FOX_HEADS // 2, 2, seq)
        o = _fox_attention(z, c_row, batch=batch, seq=seq, q_col=cols["q"], k_col=cols["k"],
                           v_col=cols["v"], tq=_pick(seq, 256))
        xs = _mixer(xs, z, o, conv_a, conv_c, row3(conv_c_bias), row3(ln_c_gain), row3(ln_c_bias), b_gate3,
                    w_oa, w_ob, w_oc, w_oo, l, cols, seq=seq, tm=_pick(seq, 256))
        xs = _xattn(xs, row3(xattn_norm), w_q, kv, w_xo_b, l, seq=seq, n_mem=n_mem, tm=_pick(seq, 512))
        xs = _ffn(xs, row3(ffn_norm), w_gu, w_dn, final_norm[None, :], l, tm=_pick(n, 512), fc=fc,
                  final=(l == depth - 1))
    return xs.reshape(batch, seq, d)
```

```python
import functools

import jax
import jax.numpy as jnp
from jax import lax
from jax.experimental import pallas as pl
from jax.experimental.pallas import tpu as pltpu

EPS = 1e-6
FOX_HEADS = 16
FOX_HEAD_DIM = 64
X_HEADS = 4
X_HEAD_DIM = 128
CONV_A_WIDTH = 3
CONV_C_WIDTH = 31
LANES = 128
HALO = 32
CUM_BLOCK = 256
VMEM_LIMIT = 56 * 1024 * 1024
SUBLANES = 8
LOG2E = 1.4426950408889634

F32 = jnp.float32
BF16 = jnp.bfloat16


def _params(*sem):
    return pltpu.CompilerParams(dimension_semantics=sem, vmem_limit_bytes=VMEM_LIMIT)


def _rms_norm(x, g):
    return x * lax.rsqrt(jnp.mean(x * x, axis=-1, keepdims=True) + EPS) * g


def _sigmoid(x):
    return 0.5 * jnp.tanh(0.5 * x) + 0.5


def _dot(a, b):
    return jnp.dot(a, b, preferred_element_type=F32)


def _dot_nt(a, b):
    return lax.dot_general(a, b, (((1,), (1,)), ((), ())), preferred_element_type=F32)


def _in_proj_kernel(x_ref, g_ref, w_ref, wf_ref, cs_ref, z_ref, f_ref, xn_ref):
    @pl.when(pl.program_id(1) == 0)
    def _():
        xn = _rms_norm(x_ref[...], g_ref[...]).astype(BF16)
        xn_ref[...] = xn
        f_ref[...] = _dot(xn, wf_ref[...])

    z_ref[...] = (_dot(xn_ref[...], w_ref[...]) * cs_ref[...]).astype(z_ref.dtype)


def _in_proj(x, gain, w_main, w_f, col_scale, layer, *, tm, tn):
    n, d = x.shape
    e = w_main.shape[-1]
    return pl.pallas_call(
        _in_proj_kernel,
        grid=(n // tm, e // tn),
        in_specs=[
            pl.BlockSpec((tm, d), lambda i, j: (i, 0)),
            pl.BlockSpec((None, 1, d), lambda i, j: (layer, 0, 0)),
            pl.BlockSpec((None, d, tn), lambda i, j: (layer, 0, j)),
            pl.BlockSpec((None, d, LANES), lambda i, j: (layer, 0, 0)),
            pl.BlockSpec((1, tn), lambda i, j: (0, j)),
        ],
        out_specs=[
            pl.BlockSpec((tm, tn), lambda i, j: (i, j)),
            pl.BlockSpec((tm, LANES), lambda i, j: (i, 0)),
        ],
        out_shape=[
            jax.ShapeDtypeStruct((n, e), BF16),
            jax.ShapeDtypeStruct((n, LANES), F32),
        ],
        scratch_shapes=[pltpu.VMEM((tm, d), BF16)],
        compiler_params=_params("parallel", "arbitrary"),
        name="in_proj",
    )(x, gain, w_main, w_f, col_scale)


def _forget_cumsum_kernel(f_ref, b_ref, tri_ref, crow_ref, c_scr):
    x = f_ref[...] + b_ref[...]
    lf = jnp.minimum(x, 0.0) - jnp.log(1.0 + jnp.exp(-jnp.abs(x)))
    hi = lf.astype(BF16)
    rem = lf - hi.astype(F32)
    mid = rem.astype(BF16)
    lo = (rem - mid.astype(F32)).astype(BF16)
    tri = tri_ref[...]
    seq = f_ref.shape[0]
    carry = jnp.zeros((1, LANES), F32)
    for blk in range(seq // CUM_BLOCK):
        sl = slice(blk * CUM_BLOCK, (blk + 1) * CUM_BLOCK)
        part = _dot(tri, hi[sl]) + _dot(tri, mid[sl]) + _dot(tri, lo[sl]) + carry
        c_scr[sl, :] = part
        carry = part[CUM_BLOCK - 1:CUM_BLOCK, :]
    crow_ref[...] = c_scr[...].T[:FOX_HEADS, :]


def _forget_cumsum(f, b_forget, layer, tri, *, batch, seq):
    return pl.pallas_call(
        _forget_cumsum_kernel,
        grid=(batch,),
        in_specs=[
            pl.BlockSpec((seq, LANES), lambda b: (b, 0)),
            pl.BlockSpec((None, 1, LANES), lambda b: (layer, 0, 0)),
            pl.BlockSpec((CUM_BLOCK, CUM_BLOCK), lambda b: (0, 0)),
        ],
        out_specs=pl.BlockSpec((None, FOX_HEADS, seq), lambda b: (b, 0, 0)),
        out_shape=jax.ShapeDtypeStruct((batch, FOX_HEADS, seq), F32),
        scratch_shapes=[pltpu.VMEM((seq, LANES), F32)],
        compiler_params=_params("parallel"),
        name="forget_cumsum",
    )(f, b_forget, tri)


def _fox_kernel(q_ref, k_ref, v_ref, crow_ref, o_ref, *, tq):
    qi = pl.program_id(2)
    lane = lax.broadcasted_iota(jnp.int32, (1, LANES), 1)
    q = q_ref[...]
    zero = jnp.zeros_like(q)
    qs = (jnp.where(lane < FOX_HEAD_DIM, q, zero), jnp.where(lane < FOX_HEAD_DIM, zero, q))

    def step(kb, carry, diagonal):
        k0 = pl.multiple_of(kb * tq, tq)
        kblk = k_ref[pl.ds(k0, tq), :]
        vblk = v_ref[pl.ds(k0, tq), :]
        new = []
        for h in range(2):
            m, l, acc = carry[h]
            s = _dot_nt(qs[h], kblk) - crow_ref[h:h + 1, pl.ds(k0, tq)] * LOG2E
            if diagonal:
                row = lax.broadcasted_iota(jnp.int32, (tq, tq), 0)
                col = lax.broadcasted_iota(jnp.int32, (tq, tq), 1)
                s = jnp.where(row >= col, s, -jnp.inf)
            m_new = jnp.maximum(m, jnp.max(s, axis=-1, keepdims=True))
            alpha = jnp.exp2(m - m_new)
            p = jnp.exp2(s - m_new)
            l = alpha * l + jnp.sum(p, axis=-1, keepdims=True)
            acc = alpha * acc + _dot(p.astype(BF16), vblk)
            new.append((m_new, l, acc))
        return tuple(new)

    init = (jnp.full((tq, 1), -jnp.inf, F32), jnp.zeros((tq, 1), F32), jnp.zeros((tq, LANES), F32))
    carry = lax.fori_loop(0, qi, functools.partial(step, diagonal=False), (init, init))
    (_, l0, acc0), (_, l1, acc1) = step(qi, carry, diagonal=True)
    o_ref[...] = jnp.where(lane < FOX_HEAD_DIM, acc0 / l0, acc1 / l1).astype(o_ref.dtype)


def _fox_attention(z, c_row, *, batch, seq, q_col, k_col, v_col, tq):
    n = z.shape[0]
    pairs = FOX_HEADS // 2
    nq = seq // tq
    qb, kb, vb = q_col // LANES, k_col // LANES, v_col // LANES
    return pl.pallas_call(
        functools.partial(_fox_kernel, tq=tq),
        grid=(batch, pairs, nq),
        in_specs=[
            pl.BlockSpec((tq, LANES), lambda b, p, i: (b * nq + i, qb + p)),
            pl.BlockSpec((seq, LANES), lambda b, p, i: (b, kb + p)),
            pl.BlockSpec((seq, LANES), lambda b, p, i: (b, vb + p)),
            pl.BlockSpec((None, None, 2, seq), lambda b, p, i: (b, p, 0, 0)),
        ],
        out_specs=pl.BlockSpec((tq, LANES), lambda b, p, i: (b * nq + i, p)),
        out_shape=jax.ShapeDtypeStruct((n, pairs * LANES), BF16),
        compiler_params=_params("parallel", "parallel", "arbitrary"),
        name="fox_attention",
    )(z, z, z, c_row)


def _shift_rows(x3, r):
    sub = lax.broadcasted_iota(jnp.int32, x3.shape[1:], 0)[None]
    rolled = pltpu.roll(x3, SUBLANES - r, axis=1)
    return jnp.where(sub < SUBLANES - r, rolled[:-1], rolled[1:])


def _mixer_kernel(x_ref, ab_ref, ac_ref, au_ref, cv_ref, cg_ref, ga_ref, gb_ref, gc_ref,
                  hac_ref, hau_ref, hcv_ref, hcg_ref, o_ref,
                  conva_ref, convc_ref, cbias_ref, lng_ref, lnb_ref, bgate_ref,
                  woa_ref, wob_ref, woc_ref, wo_ref, out_ref,
                  ush_scr, p_scr, uc_scr, pa_scr, wc_scr, wa_scr, *, tm, seq, rows):
    d = x_ref.shape[-1]
    hg = HALO // SUBLANES
    gu = hg + tm // SUBLANES
    gc = rows // SUBLANES

    def grouped(v):
        return v.reshape(v.shape[0] // SUBLANES, SUBLANES, d)

    for k in range(CONV_C_WIDTH):
        wc_scr[k] = jnp.broadcast_to(convc_ref[k:k + 1, :], (SUBLANES, d))
    for k in range(CONV_A_WIDTH):
        wa_scr[k] = jnp.broadcast_to(conva_ref[k:k + 1, :], (SUBLANES, d))

    keep = jnp.where((pl.program_id(0) * tm) % seq == 0, 0.0, 1.0)
    ush_scr[0, 0:hg] = grouped(hcv_ref[...].astype(F32) * _sigmoid(hcg_ref[...].astype(F32)) * keep)
    ush_scr[0, hg:] = grouped(cv_ref[...].astype(F32) * _sigmoid(cg_ref[...].astype(F32)))
    p_scr[0:hg] = grouped(hac_ref[...].astype(F32) * hau_ref[...].astype(F32) * keep)
    p_scr[hg:] = grouped(ac_ref[...].astype(F32) * au_ref[...].astype(F32))
    u0 = ush_scr[0]
    for r in range(1, SUBLANES):
        ush_scr[r, 0:gu - 1] = _shift_rows(u0, r)

    for c in range(tm // rows):
        g0 = c * gc
        r0 = c * rows
        acc = jnp.broadcast_to(cbias_ref[...][None], (gc, SUBLANES, d))
        for k in range(CONV_C_WIDTH):
            a, r = divmod(HALO - (CONV_C_WIDTH - 1) + k, SUBLANES)
            acc = acc + wc_scr[k][None] * ush_scr[r, g0 + a:g0 + a + gc]
        mu = jnp.mean(acc, axis=-1, keepdims=True)
        xc = acc - mu
        y = xc * lax.rsqrt(jnp.mean(xc * xc, axis=-1, keepdims=True) + EPS) * lng_ref[...][None] + lnb_ref[...][None]
        uc_scr[r0:r0 + rows, :] = (y * _sigmoid(y)).reshape(rows, d).astype(BF16)
        acc = None
        for k in range(CONV_A_WIDTH):
            a, r = divmod(HALO - (CONV_A_WIDTH - 1) + k, SUBLANES)
            if r == 0:
                tap = p_scr[g0 + a:g0 + a + gc]
            else:
                tap = _shift_rows(p_scr[g0 + a:g0 + a + gc + 1], r)
            term = wa_scr[k][None] * tap
            acc = term if acc is None else acc + term
        pa_scr[r0:r0 + rows, :] = (ab_ref[r0:r0 + rows, :].astype(F32) * acc.reshape(rows, d)).astype(BF16)

    bg = bgate_ref[...]
    merged = _sigmoid(ga_ref[...].astype(F32) + bg[0:1, :]) * _dot(pa_scr[...], woa_ref[...])
    merged = merged + _sigmoid(gb_ref[...].astype(F32) + bg[1:2, :]) * _dot(o_ref[...], wob_ref[...])
    merged = merged + _sigmoid(gc_ref[...].astype(F32) + bg[2:3, :]) * _dot(uc_scr[...], woc_ref[...])
    out_ref[...] = x_ref[...] + _dot(merged.astype(BF16), wo_ref[...])


def _mixer(x, z, o, conv_a, conv_c, conv_c_bias, ln_g, ln_b, b_gate, w_out_a, w_out_b, w_out_c, w_o,
           layer, cols, *, seq, tm, rows=32):
    n, d = x.shape
    hb = tm // HALO
    cb = {name: c // d for name, c in cols.items()}

    def zspec(name):
        j = cb[name]
        return pl.BlockSpec((tm, d), lambda i: (i, j))

    def hspec(name):
        j = cb[name]
        return pl.BlockSpec((HALO, d), lambda i: (jnp.maximum(i * hb - 1, 0), j))

    def vec(k):
        return pl.BlockSpec((None, k, d), lambda i: (layer, 0, 0))

    wspec = pl.BlockSpec((None, d, d), lambda i: (layer, 0, 0))
    tile = pl.BlockSpec((tm, d), lambda i: (i, 0))
    return pl.pallas_call(
        functools.partial(_mixer_kernel, tm=tm, seq=seq, rows=rows),
        grid=(n // tm,),
        in_specs=[tile] + [zspec(k) for k in ("a_b", "a_c", "a_u", "c_val", "c_gate", "g_a", "g_b", "g_c")]
        + [hspec(k) for k in ("a_c", "a_u", "c_val", "c_gate")] + [tile]
        + [vec(CONV_A_WIDTH), vec(CONV_C_WIDTH), vec(1), vec(1), vec(1), vec(3)]
        + [wspec] * 4,
        out_specs=tile,
        out_shape=jax.ShapeDtypeStruct((n, d), F32),
        scratch_shapes=[
            pltpu.VMEM((SUBLANES, (tm + HALO) // SUBLANES, SUBLANES, d), F32),
            pltpu.VMEM(((tm + HALO) // SUBLANES, SUBLANES, d), F32),
            pltpu.VMEM((tm, d), BF16),
            pltpu.VMEM((tm, d), BF16),
            pltpu.VMEM((CONV_C_WIDTH, SUBLANES, d), F32),
            pltpu.VMEM((CONV_A_WIDTH, SUBLANES, d), F32),
        ],
        compiler_params=_params("parallel"),
        name="mixer_merge",
    )(x, *([z] * 12), o, conv_a, conv_c, conv_c_bias, ln_g, ln_b, b_gate, w_out_a, w_out_b, w_out_c, w_o)


def _mem_kv_kernel(mem_ref, g_ref, w_ref, kv_ref):
    kv_ref[...] = _dot(_rms_norm(mem_ref[...], g_ref[...]).astype(BF16), w_ref[...]).astype(kv_ref.dtype)


def _mem_kv(mem, gain, w_xkv):
    m, d = mem.shape
    depth, _, e = w_xkv.shape
    return pl.pallas_call(
        _mem_kv_kernel,
        grid=(depth,),
        in_specs=[
            pl.BlockSpec((m, d), lambda l: (0, 0)),
            pl.BlockSpec((1, d), lambda l: (0, 0)),
            pl.BlockSpec((None, d, e), lambda l: (l, 0, 0)),
        ],
        out_specs=pl.BlockSpec((None, m, e), lambda l: (l, 0, 0)),
        out_shape=jax.ShapeDtypeStruct((depth, m, e), BF16),
        compiler_params=_params("parallel"),
        name="mem_kv",
    )(mem, gain, w_xkv)


def _xattn_kernel(x_ref, g_ref, wq_ref, kv_ref, wo_ref, out_ref):
    x = x_ref[...]
    q = _dot(_rms_norm(x, g_ref[...]).astype(BF16), wq_ref[...]) * (X_HEAD_DIM ** -0.5)
    q = q.astype(BF16)
    dx = X_HEADS * X_HEAD_DIM
    outs = []
    for h in range(X_HEADS):
        sl = slice(h * X_HEAD_DIM, (h + 1) * X_HEAD_DIM)
        s = _dot_nt(q[:, sl], kv_ref[:, sl])
        p = jnp.exp(s - jnp.max(s, axis=-1, keepdims=True))
        oh = _dot(p.astype(BF16), kv_ref[:, dx + h * X_HEAD_DIM:dx + (h + 1) * X_HEAD_DIM])
        outs.append((oh / jnp.sum(p, axis=-1, keepdims=True)).astype(BF16))
    out_ref[...] = x + _dot(jnp.concatenate(outs, axis=-1), wo_ref[...])


def _xattn(x, gain, w_xq, kv, w_xo, layer, *, seq, n_mem, tm):
    n, d = x.shape
    dx = w_xq.shape[-1]
    per_seq = seq // tm
    tile = pl.BlockSpec((tm, d), lambda i: (i, 0))
    return pl.pallas_call(
        _xattn_kernel,
        grid=(n // tm,),
        in_specs=[
            tile,
            pl.BlockSpec((None, 1, d), lambda i: (layer, 0, 0)),
            pl.BlockSpec((None, d, dx), lambda i: (layer, 0, 0)),
            pl.BlockSpec((None, None, n_mem, 2 * dx), lambda i: (layer, i // per_seq, 0, 0)),
            pl.BlockSpec((None, dx, d), lambda i: (layer, 0, 0)),
        ],
        out_specs=tile,
        out_shape=jax.ShapeDtypeStruct((n, d), F32),
        compiler_params=_params("parallel"),
        name="xattn",
    )(x, gain, w_xq, kv, w_xo)


def _ffn_kernel(x_ref, g_ref, wg_ref, wu_ref, wd_ref, fg_ref, out_ref, xn_ref, *, final):
    j = pl.program_id(1)

    @pl.when(j == 0)
    def _():
        xn_ref[...] = _rms_norm(x_ref[...], g_ref[...]).astype(BF16)

    xn = xn_ref[...]
    g = _dot(xn, wg_ref[...])
    h = (g * _sigmoid(g) * _dot(xn, wu_ref[...])).astype(BF16)
    y = _dot(h, wd_ref[...])

    @pl.when(j == 0)
    def _():
        out_ref[...] = x_ref[...] + y

    @pl.when(j > 0)
    def _():
        out_ref[...] += y

    if final:
        @pl.when(j == pl.num_programs(1) - 1)
        def _():
            out_ref[...] = _rms_norm(out_ref[...], fg_ref[...])


def _ffn(x, gain, w_gate_up, w_down, final_gain, layer, *, tm, fc, final):
    n, d = x.shape
    dff = w_down.shape[1]
    nj = dff // fc
    tile = pl.BlockSpec((tm, d), lambda i, j: (i, 0))
    return pl.pallas_call(
        functools.partial(_ffn_kernel, final=final),
        grid=(n // tm, nj),
        in_specs=[
            tile,
            pl.BlockSpec((None, 1, d), lambda i, j: (layer, 0, 0)),
            pl.BlockSpec((None, d, fc), lambda i, j: (layer, 0, j)),
            pl.BlockSpec((None, d, fc), lambda i, j: (layer, 0, nj + j)),
            pl.BlockSpec((None, fc, d), lambda i, j: (layer, j, 0)),
            pl.BlockSpec((1, d), lambda i, j: (0, 0)),
        ],
        out_specs=tile,
        out_shape=jax.ShapeDtypeStruct((n, d), F32),
        scratch_shapes=[pltpu.VMEM((tm, d), BF16)],
        compiler_params=_params("parallel", "arbitrary"),
        name="ffn",
    )(x, gain, w_gate_up, w_gate_up, w_down, final_gain)


def _pick(n, pref):
    t = min(n, pref)
    assert n % t == 0, (n, t)
    return t


def kernel(x, mem, mix_norm, w_in, b_gate, b_forget, conv_a, w_out_a, w_out_b, conv_c, conv_c_bias,
           ln_c_gain, ln_c_bias, w_out_c, w_o, xattn_norm, mem_norm, w_xq, w_xkv, w_xo, ffn_norm,
           w_gate_up, w_down, final_norm):
    batch, seq, d = x.shape
    depth = w_in.shape[0]
    n_mem = mem.shape[1]
    n = batch * seq
    d_b = FOX_HEADS * FOX_HEAD_DIM
    assert d % LANES == 0 and seq % CUM_BLOCK == 0 and d_b == d

    names = ("a_b", "a_c", "a_u", "q", "k", "v", "c_val", "c_gate", "g_a", "g_b", "g_c")
    cols = {name: idx * d for idx, name in enumerate(names)}
    f_col = 6 * d
    w_main = jnp.concatenate([w_in[:, :, :f_col], w_in[:, :, f_col + FOX_HEADS:]], axis=-1).astype(BF16)
    w_f = jnp.pad(w_in[:, :, f_col:f_col + FOX_HEADS], ((0, 0), (0, 0), (0, LANES - FOX_HEADS))).astype(BF16)
    b_f = jnp.pad(b_forget, ((0, 0), (0, LANES - FOX_HEADS)))[:, None, :]
    tri = jnp.tril(jnp.ones((CUM_BLOCK, CUM_BLOCK), BF16))
    col_scale = jnp.ones((1, w_main.shape[-1]), F32).at[:, cols["q"]:cols["q"] + d_b].set(
        FOX_HEAD_DIM ** -0.5 * LOG2E)

    row3 = lambda a: a[:, None, :]
    bf = lambda a: a.astype(BF16)
    b_gate3 = b_gate.reshape(depth, 3, d)

    kv = _mem_kv(mem.reshape(batch * n_mem, d), mem_norm[None, :], bf(w_xkv))
    kv = kv.reshape(depth, batch, n_mem, kv.shape[-1])
    w_oa, w_ob, w_oc, w_oo = bf(w_out_a), bf(w_out_b), bf(w_out_c), bf(w_o)
    w_q, w_xo_b, w_gu, w_dn = bf(w_xq), bf(w_xo), bf(w_gate_up), bf(w_down)

    dff = w_down.shape[1]
    fc = dff // 2 if (dff // 2) % LANES == 0 else dff
    xs = x.reshape(n, d)
    for l in range(depth):
        z, f = _in_proj(xs, row3(mix_norm), w_main, w_f, col_scale, l, tm=_pick(n, 1024),
                        tn=_pick(w_main.shape[-1], 1024))
        c_row = _forget_cumsum(f, b_f, l, tri, batch=batch, seq=seq)
        c_row = c_row.reshape(batch, FOX_HEADS // 2, 2, seq)
        o = _fox_attention(z, c_row, batch=batch, seq=seq, q_col=cols["q"], k_col=cols["k"],
                           v_col=cols["v"], tq=_pick(seq, 512))
        xs = _mixer(xs, z, o, conv_a, conv_c, row3(conv_c_bias), row3(ln_c_gain), row3(ln_c_bias), b_gate3,
                    w_oa, w_ob, w_oc, w_oo, l, cols, seq=seq, tm=_pick(seq, 256))
        xs = _xattn(xs, row3(xattn_norm), w_q, kv, w_xo_b, l, seq=seq, n_mem=n_mem, tm=_pick(seq, 512))
        xs = _ffn(xs, row3(ffn_norm), w_gu, w_dn, final_norm[None, :], l, tm=_pick(n, 512), fc=fc,
                  final=(l == depth - 1))
    return xs.reshape(batch, seq, d)
```

```python
import functools

import jax
import jax.numpy as jnp
from jax import lax
from jax.experimental import pallas as pl
from jax.experimental.pallas import tpu as pltpu

EPS = 1e-6
FOX_HEADS = 16
FOX_HEAD_DIM = 64
X_HEADS = 4
X_HEAD_DIM = 128
CONV_A_WIDTH = 3
CONV_C_WIDTH = 31
LANES = 128
HALO = 32
MXU_WIDTH = 256
CUM_BLOCK = MXU_WIDTH
VMEM_LIMIT = 56 * 1024 * 1024
SUBLANES = 8
LOG2E = 1.4426950408889634

F32 = jnp.float32
BF16 = jnp.bfloat16


def _params(*sem):
    return pltpu.CompilerParams(dimension_semantics=sem, vmem_limit_bytes=VMEM_LIMIT)


def _rms_norm(x, g):
    return x * lax.rsqrt(jnp.mean(x * x, axis=-1, keepdims=True) + EPS) * g


def _sigmoid(x):
    return 0.5 * jnp.tanh(0.5 * x) + 0.5


def _dot(a, b):
    return jnp.dot(a, b, preferred_element_type=F32)


def _dot_nt(a, b):
    return lax.dot_general(a, b, (((1,), (1,)), ((), ())), preferred_element_type=F32)


def _in_proj_kernel(x_ref, g_ref, w_ref, wf_ref, cs_ref, cb_ref, z_ref, f_ref, xn_ref, *, gate_tile):
    j = pl.program_id(1)

    @pl.when(j == 0)
    def _():
        xn = _rms_norm(x_ref[...], g_ref[...]).astype(BF16)
        xn_ref[...] = xn
        f_ref[...] = _dot(xn, wf_ref[...])

    @pl.when(j < gate_tile)
    def _():
        z_ref[...] = (_dot(xn_ref[...], w_ref[...]) * cs_ref[...]).astype(z_ref.dtype)

    @pl.when(j >= gate_tile)
    def _():
        z_ref[...] = _sigmoid(_dot(xn_ref[...], w_ref[...]) + cb_ref[...]).astype(z_ref.dtype)


def _in_proj(x, gain, w_main, w_f, col_scale, col_bias, layer, *, tm, tn, gate_col):
    n, d = x.shape
    e = w_main.shape[-1]
    return pl.pallas_call(
        functools.partial(_in_proj_kernel, gate_tile=gate_col // tn),
        grid=(n // tm, e // tn),
        in_specs=[
            pl.BlockSpec((tm, d), lambda i, j: (i, 0)),
            pl.BlockSpec((None, 1, d), lambda i, j: (layer, 0, 0)),
            pl.BlockSpec((None, d, tn), lambda i, j: (layer, 0, j)),
            pl.BlockSpec((None, d, LANES), lambda i, j: (layer, 0, 0)),
            pl.BlockSpec((1, tn), lambda i, j: (0, j)),
            pl.BlockSpec((None, 1, tn), lambda i, j: (layer, 0, j)),
        ],
        out_specs=[
            pl.BlockSpec((tm, tn), lambda i, j: (i, j)),
            pl.BlockSpec((tm, LANES), lambda i, j: (i, 0)),
        ],
        out_shape=[
            jax.ShapeDtypeStruct((n, e), BF16),
            jax.ShapeDtypeStruct((n, LANES), F32),
        ],
        scratch_shapes=[pltpu.VMEM((tm, d), BF16)],
        compiler_params=_params("parallel", "arbitrary"),
        name="in_proj",
    )(x, gain, w_main, w_f, col_scale, col_bias)


def _forget_cumsum_kernel(f_ref, b_ref, tri_ref, crow_ref, c_scr):
    x = f_ref[...] + b_ref[...]
    lf = jnp.minimum(x, 0.0) - jnp.log(1.0 + jnp.exp(-jnp.abs(x)))
    hi = lf.astype(BF16)
    rem = lf - hi.astype(F32)
    mid = rem.astype(BF16)
    lo = (rem - mid.astype(F32)).astype(BF16)
    tri = tri_ref[...]
    seq = f_ref.shape[0]
    carry = jnp.zeros((1, LANES), F32)
    for blk in range(seq // CUM_BLOCK):
        sl = slice(blk * CUM_BLOCK, (blk + 1) * CUM_BLOCK)
        part = _dot(tri, hi[sl]) + _dot(tri, mid[sl]) + _dot(tri, lo[sl]) + carry
        c_scr[sl, :] = part
        carry = part[CUM_BLOCK - 1:CUM_BLOCK, :]
    crow_ref[...] = c_scr[...].T[:FOX_HEADS, :]


def _forget_cumsum(f, b_forget, layer, tri, *, batch, seq):
    return pl.pallas_call(
        _forget_cumsum_kernel,
        grid=(batch,),
        in_specs=[
            pl.BlockSpec((seq, LANES), lambda b: (b, 0)),
            pl.BlockSpec((None, 1, LANES), lambda b: (layer, 0, 0)),
            pl.BlockSpec((CUM_BLOCK, CUM_BLOCK), lambda b: (0, 0)),
        ],
        out_specs=pl.BlockSpec((None, FOX_HEADS, seq), lambda b: (b, 0, 0)),
        out_shape=jax.ShapeDtypeStruct((batch, FOX_HEADS, seq), F32),
        scratch_shapes=[pltpu.VMEM((seq, LANES), F32)],
        compiler_params=_params("parallel"),
        name="forget_cumsum",
    )(f, b_forget, tri)


def _fox_kernel(q_ref, k_ref, v_ref, crow_ref, o_ref, *, tq):
    lane = lax.broadcasted_iota(jnp.int32, (1, LANES), 1)

    def run(nblocks):
        q = q_ref[...]
        zero = jnp.zeros_like(q)
        qs = (jnp.where(lane < FOX_HEAD_DIM, q, zero), jnp.where(lane < FOX_HEAD_DIM, zero, q))
        m = [jnp.full((tq, 1), -jnp.inf, F32)] * 2
        l = [jnp.zeros((tq, 1), F32)] * 2
        acc = [jnp.zeros((tq, LANES), F32)] * 2
        for j in range(nblocks):
            keys = slice(j * tq, (j + 1) * tq)
            kblk = k_ref[keys, :]
            vblk = v_ref[keys, :]
            for h in range(2):
                s = _dot_nt(qs[h], kblk) - crow_ref[h:h + 1, keys] * LOG2E
                if j == nblocks - 1:
                    row = lax.broadcasted_iota(jnp.int32, (tq, tq), 0)
                    col = lax.broadcasted_iota(jnp.int32, (tq, tq), 1)
                    s = jnp.where(row >= col, s, -jnp.inf)
                m_new = jnp.maximum(m[h], jnp.max(s, axis=-1, keepdims=True))
                alpha = jnp.exp2(m[h] - m_new)
                p = jnp.exp2(s - m_new)
                l[h] = alpha * l[h] + jnp.sum(p, axis=-1, keepdims=True)
                m[h] = m_new
                acc[h] = alpha * acc[h] + _dot(p.astype(BF16), vblk)
        o_ref[...] = jnp.where(lane < FOX_HEAD_DIM, acc[0] / l[0], acc[1] / l[1]).astype(o_ref.dtype)

    qi = pl.program_id(2)
    for n in range(k_ref.shape[0] // tq):
        pl.when(qi == n)(functools.partial(run, n + 1))


def _fox_attention(z, c_row, *, batch, seq, q_col, k_col, v_col, tq):
    n = z.shape[0]
    pairs = FOX_HEADS // 2
    nq = seq // tq
    qb, kb, vb = q_col // LANES, k_col // LANES, v_col // LANES
    return pl.pallas_call(
        functools.partial(_fox_kernel, tq=tq),
        grid=(batch, pairs, nq),
        in_specs=[
            pl.BlockSpec((tq, LANES), lambda b, p, i: (b * nq + i, qb + p)),
            pl.BlockSpec((seq, LANES), lambda b, p, i: (b, kb + p)),
            pl.BlockSpec((seq, LANES), lambda b, p, i: (b, vb + p)),
            pl.BlockSpec((None, None, 2, seq), lambda b, p, i: (b, p, 0, 0)),
        ],
        out_specs=pl.BlockSpec((tq, LANES), lambda b, p, i: (b * nq + i, p)),
        out_shape=jax.ShapeDtypeStruct((n, pairs * LANES), BF16),
        compiler_params=_params("parallel", "parallel", "arbitrary"),
        name="fox_attention",
    )(z, z, z, c_row)


def _shift_rows(x3, r):
    sub = lax.broadcasted_iota(jnp.int32, x3.shape[1:], 0)[None]
    rolled = pltpu.roll(x3, SUBLANES - r, axis=1)
    return jnp.where(sub < SUBLANES - r, rolled[:-1], rolled[1:])


def _mixer_kernel(x_ref, ab_ref, ac_ref, au_ref, cv_ref, cg_ref, ga_ref, gb_ref, gc_ref,
                  hac_ref, hau_ref, hcv_ref, hcg_ref, o_ref,
                  conva_ref, convc_ref, cbias_ref, lng_ref, lnb_ref,
                  woa_ref, wob_ref, woc_ref, wo_ref, out_ref,
                  ush_scr, p_scr, uc_scr, pa_scr, wc_scr, wa_scr, *, tm, seq, rows):
    d = x_ref.shape[-1]
    hg = HALO // SUBLANES
    gu = hg + tm // SUBLANES
    gc = rows // SUBLANES

    def grouped(v):
        return v.reshape(v.shape[0] // SUBLANES, SUBLANES, d)

    for k in range(CONV_C_WIDTH):
        wc_scr[k] = jnp.broadcast_to(convc_ref[k:k + 1, :], (SUBLANES, d))
    for k in range(CONV_A_WIDTH):
        wa_scr[k] = jnp.broadcast_to(conva_ref[k:k + 1, :], (SUBLANES, d))

    keep = jnp.where((pl.program_id(0) * tm) % seq == 0, 0.0, 1.0)
    ush_scr[0, 0:hg] = grouped(hcv_ref[...].astype(F32) * hcg_ref[...].astype(F32) * keep)
    ush_scr[0, hg:] = grouped(cv_ref[...].astype(F32) * cg_ref[...].astype(F32))
    p_scr[0:hg] = grouped(hac_ref[...].astype(F32) * hau_ref[...].astype(F32) * keep)
    p_scr[hg:] = grouped(ac_ref[...].astype(F32) * au_ref[...].astype(F32))
    u0 = ush_scr[0]
    for r in range(1, SUBLANES):
        ush_scr[r, 0:gu - 1] = _shift_rows(u0, r)

    for c in range(tm // rows):
        g0 = c * gc
        r0 = c * rows
        acc = jnp.broadcast_to(cbias_ref[...][None], (gc, SUBLANES, d))
        for k in range(CONV_C_WIDTH):
            a, r = divmod(HALO - (CONV_C_WIDTH - 1) + k, SUBLANES)
            acc = acc + wc_scr[k][None] * ush_scr[r, g0 + a:g0 + a + gc]
        mu = jnp.mean(acc, axis=-1, keepdims=True)
        xc = acc - mu
        y = xc * lax.rsqrt(jnp.mean(xc * xc, axis=-1, keepdims=True) + EPS) * lng_ref[...][None] + lnb_ref[...][None]
        uc_scr[r0:r0 + rows, :] = (y * _sigmoid(y)).reshape(rows, d).astype(BF16)
        acc = None
        for k in range(CONV_A_WIDTH):
            a, r = divmod(HALO - (CONV_A_WIDTH - 1) + k, SUBLANES)
            if r == 0:
                tap = p_scr[g0 + a:g0 + a + gc]
            else:
                tap = _shift_rows(p_scr[g0 + a:g0 + a + gc + 1], r)
            term = wa_scr[k][None] * tap
            acc = term if acc is None else acc + term
        pa_scr[r0:r0 + rows, :] = (ab_ref[r0:r0 + rows, :].astype(F32) * acc.reshape(rows, d)).astype(BF16)

    merged = ga_ref[...].astype(F32) * _dot(pa_scr[...], woa_ref[...])
    merged = merged + gb_ref[...].astype(F32) * _dot(o_ref[...], wob_ref[...])
    merged = merged + gc_ref[...].astype(F32) * _dot(uc_scr[...], woc_ref[...])
    out_ref[...] = x_ref[...] + _dot(merged.astype(BF16), wo_ref[...])


def _mixer(x, z, o, conv_a, conv_c, conv_c_bias, ln_g, ln_b, w_out_a, w_out_b, w_out_c, w_o,
           layer, cols, *, seq, tm, rows=32):
    n, d = x.shape
    hb = tm // HALO
    cb = {name: c // d for name, c in cols.items()}

    def zspec(name):
        j = cb[name]
        return pl.BlockSpec((tm, d), lambda i: (i, j))

    def hspec(name):
        j = cb[name]
        return pl.BlockSpec((HALO, d), lambda i: (jnp.maximum(i * hb - 1, 0), j))

    def vec(k):
        return pl.BlockSpec((None, k, d), lambda i: (layer, 0, 0))

    wspec = pl.BlockSpec((None, d, d), lambda i: (layer, 0, 0))
    tile = pl.BlockSpec((tm, d), lambda i: (i, 0))
    return pl.pallas_call(
        functools.partial(_mixer_kernel, tm=tm, seq=seq, rows=rows),
        grid=(n // tm,),
        in_specs=[tile] + [zspec(k) for k in ("a_b", "a_c", "a_u", "c_val", "c_gate", "g_a", "g_b", "g_c")]
        + [hspec(k) for k in ("a_c", "a_u", "c_val", "c_gate")] + [tile]
        + [vec(CONV_A_WIDTH), vec(CONV_C_WIDTH), vec(1), vec(1), vec(1)]
        + [wspec] * 4,
        out_specs=tile,
        out_shape=jax.ShapeDtypeStruct((n, d), F32),
        scratch_shapes=[
            pltpu.VMEM((SUBLANES, (tm + HALO) // SUBLANES, SUBLANES, d), F32),
            pltpu.VMEM(((tm + HALO) // SUBLANES, SUBLANES, d), F32),
            pltpu.VMEM((tm, d), BF16),
            pltpu.VMEM((tm, d), BF16),
            pltpu.VMEM((CONV_C_WIDTH, SUBLANES, d), F32),
            pltpu.VMEM((CONV_A_WIDTH, SUBLANES, d), F32),
        ],
        compiler_params=_params("parallel"),
        name="mixer_merge",
    )(x, *([z] * 12), o, conv_a, conv_c, conv_c_bias, ln_g, ln_b, w_out_a, w_out_b, w_out_c, w_o)


def _mem_kv_kernel(mem_ref, g_ref, w_ref, kv_ref):
    kv_ref[...] = _dot(_rms_norm(mem_ref[...], g_ref[...]).astype(BF16), w_ref[...]).astype(kv_ref.dtype)


def _mem_kv(mem, gain, w_xkv):
    m, d = mem.shape
    depth, _, e = w_xkv.shape
    return pl.pallas_call(
        _mem_kv_kernel,
        grid=(depth,),
        in_specs=[
            pl.BlockSpec((m, d), lambda l: (0, 0)),
            pl.BlockSpec((1, d), lambda l: (0, 0)),
            pl.BlockSpec((None, d, e), lambda l: (l, 0, 0)),
        ],
        out_specs=pl.BlockSpec((None, m, e), lambda l: (l, 0, 0)),
        out_shape=jax.ShapeDtypeStruct((depth, m, e), BF16),
        compiler_params=_params("parallel"),
        name="mem_kv",
    )(mem, gain, w_xkv)


def _xattn_kernel(x_ref, g_ref, wq_ref, kv_ref, wo_ref, out_ref):
    x = x_ref[...]
    q = _dot(_rms_norm(x, g_ref[...]).astype(BF16), wq_ref[...]) * (X_HEAD_DIM ** -0.5)
    q = q.astype(BF16)
    dx = X_HEADS * X_HEAD_DIM
    outs = []
    for h in range(X_HEADS):
        sl = slice(h * X_HEAD_DIM, (h + 1) * X_HEAD_DIM)
        s = _dot_nt(q[:, sl], kv_ref[:, sl])
        p = jnp.exp(s - jnp.max(s, axis=-1, keepdims=True))
        oh = _dot(p.astype(BF16), kv_ref[:, dx + h * X_HEAD_DIM:dx + (h + 1) * X_HEAD_DIM])
        outs.append((oh / jnp.sum(p, axis=-1, keepdims=True)).astype(BF16))
    out_ref[...] = x + _dot(jnp.concatenate(outs, axis=-1), wo_ref[...])


def _xattn(x, gain, w_xq, kv, w_xo, layer, *, seq, n_mem, tm):
    n, d = x.shape
    dx = w_xq.shape[-1]
    per_seq = seq // tm
    tile = pl.BlockSpec((tm, d), lambda i: (i, 0))
    return pl.pallas_call(
        _xattn_kernel,
        grid=(n // tm,),
        in_specs=[
            tile,
            pl.BlockSpec((None, 1, d), lambda i: (layer, 0, 0)),
            pl.BlockSpec((None, d, dx), lambda i: (layer, 0, 0)),
            pl.BlockSpec((None, None, n_mem, 2 * dx), lambda i: (layer, i // per_seq, 0, 0)),
            pl.BlockSpec((None, dx, d), lambda i: (layer, 0, 0)),
        ],
        out_specs=tile,
        out_shape=jax.ShapeDtypeStruct((n, d), F32),
        compiler_params=_params("parallel"),
        name="xattn",
    )(x, gain, w_xq, kv, w_xo)


def _ffn_kernel(x_ref, g_ref, wg_ref, wu_ref, wd_ref, fg_ref, out_ref, *, final, chunks):
    x = x_ref[...]
    xn = _rms_norm(x, g_ref[...]).astype(BF16)
    y = x
    for c0, c1 in chunks:
        g = _dot(xn, wg_ref[:, c0:c1])
        h = (g * _sigmoid(g) * _dot(xn, wu_ref[:, c0:c1])).astype(BF16)
        y = y + _dot(h, wd_ref[c0:c1, :])
    if final:
        y = _rms_norm(y, fg_ref[...])
    out_ref[...] = y


def _ffn_chunks(dff):
    tiles = dff // MXU_WIDTH
    assert tiles * MXU_WIDTH == dff
    half = (tiles + 1) // 2
    return tuple((a * MXU_WIDTH, b * MXU_WIDTH) for a, b in ((0, half), (half, tiles)) if b > a)


def _ffn(x, gain, w_gate_up, w_down, final_gain, layer, *, tm, final):
    n, d = x.shape
    dff = w_down.shape[1]
    tile = pl.BlockSpec((tm, d), lambda i: (i, 0))
    once = pl.Buffered(1)
    return pl.pallas_call(
        functools.partial(_ffn_kernel, final=final, chunks=_ffn_chunks(dff)),
        grid=(n // tm,),
        in_specs=[
            tile,
            pl.BlockSpec((None, 1, d), lambda i: (layer, 0, 0)),
            pl.BlockSpec((None, d, dff), lambda i: (layer, 0, 0), pipeline_mode=once),
            pl.BlockSpec((None, d, dff), lambda i: (layer, 0, 1), pipeline_mode=once),
            pl.BlockSpec((None, dff, d), lambda i: (layer, 0, 0), pipeline_mode=once),
            pl.BlockSpec((1, d), lambda i: (0, 0)),
        ],
        out_specs=tile,
        out_shape=jax.ShapeDtypeStruct((n, d), F32),
        compiler_params=_params("parallel"),
        name="ffn",
    )(x, gain, w_gate_up, w_gate_up, w_down, final_gain)


def _pick(n, pref):
    t = min(n, pref)
    assert n % t == 0, (n, t)
    return t


def kernel(x, mem, mix_norm, w_in, b_gate, b_forget, conv_a, w_out_a, w_out_b, conv_c, conv_c_bias,
           ln_c_gain, ln_c_bias, w_out_c, w_o, xattn_norm, mem_norm, w_xq, w_xkv, w_xo, ffn_norm,
           w_gate_up, w_down, final_norm):
    batch, seq, d = x.shape
    depth = w_in.shape[0]
    n_mem = mem.shape[1]
    n = batch * seq
    d_b = FOX_HEADS * FOX_HEAD_DIM
    assert d % LANES == 0 and seq % CUM_BLOCK == 0 and d_b == d

    names = ("a_b", "a_c", "a_u", "q", "k", "v", "c_val", "c_gate", "g_a", "g_b", "g_c")
    cols = {name: idx * d for idx, name in enumerate(names)}
    f_col = 6 * d
    w_main = jnp.concatenate([w_in[:, :, :f_col], w_in[:, :, f_col + FOX_HEADS:]], axis=-1).astype(BF16)
    w_f = jnp.pad(w_in[:, :, f_col:f_col + FOX_HEADS], ((0, 0), (0, 0), (0, LANES - FOX_HEADS))).astype(BF16)
    b_f = jnp.pad(b_forget, ((0, 0), (0, LANES - FOX_HEADS)))[:, None, :]
    tri = jnp.tril(jnp.ones((CUM_BLOCK, CUM_BLOCK), BF16))
    col_scale = jnp.ones((1, w_main.shape[-1]), F32).at[:, cols["q"]:cols["q"] + d_b].set(
        FOX_HEAD_DIM ** -0.5 * LOG2E)

    row3 = lambda a: a[:, None, :]
    bf = lambda a: a.astype(BF16)
    gate_col = cols["c_gate"]
    col_bias = jnp.zeros((depth, 1, w_main.shape[-1]), F32).at[:, 0, cols["g_a"]:].set(b_gate)

    kv = _mem_kv(mem.reshape(batch * n_mem, d), mem_norm[None, :], bf(w_xkv))
    kv = kv.reshape(depth, batch, n_mem, kv.shape[-1])
    w_oa, w_ob, w_oc, w_oo = bf(w_out_a), bf(w_out_b), bf(w_out_c), bf(w_o)
    w_q, w_xo_b, w_gu, w_dn = bf(w_xq), bf(w_xo), bf(w_gate_up), bf(w_down)

    xs = x.reshape(n, d)
    for l in range(depth):
        z, f = _in_proj(xs, row3(mix_norm), w_main, w_f, col_scale, col_bias, l, tm=_pick(n, 2048),
                        tn=_pick(d, 1024), gate_col=gate_col)
        c_row = _forget_cumsum(f, b_f, l, tri, batch=batch, seq=seq)
        c_row = c_row.reshape(batch, FOX_HEADS // 2, 2, seq)
        o = _fox_attention(z, c_row, batch=batch, seq=seq, q_col=cols["q"], k_col=cols["k"],
                           v_col=cols["v"], tq=_pick(seq, 512))
        xs = _mixer(xs, z, o, conv_a, conv_c, row3(conv_c_bias), row3(ln_c_gain), row3(ln_c_bias),
                    w_oa, w_ob, w_oc, w_oo, l, cols, seq=seq, tm=_pick(seq, 256))
        xs = _xattn(xs, row3(xattn_norm), w_q, kv, w_xo_b, l, seq=seq, n_mem=n_mem, tm=_pick(seq, 512))
        xs = _ffn(xs, row3(ffn_norm), w_gu, w_dn, final_norm[None, :], l, tm=_pick(n, 512),
                  final=(l == depth - 1))
    return xs.reshape(batch, seq, d)
```

```python
import functools

import jax
import jax.numpy as jnp
from jax import lax
from jax.experimental import pallas as pl
from jax.experimental.pallas import tpu as pltpu

EPS = 1e-6
FOX_HEADS = 16
FOX_HEAD_DIM = 64
X_HEADS = 4
X_HEAD_DIM = 128
CONV_A_WIDTH = 3
CONV_C_WIDTH = 31
LANES = 128
HALO = 32
MXU_WIDTH = 256
CUM_BLOCK = MXU_WIDTH
VMEM_LIMIT = 56 * 1024 * 1024
SUBLANES = 8
LOG2E = 1.4426950408889634

F32 = jnp.float32
BF16 = jnp.bfloat16


def _params(*sem):
    return pltpu.CompilerParams(dimension_semantics=sem, vmem_limit_bytes=VMEM_LIMIT)


def _rms_norm(x, g):
    return x * lax.rsqrt(jnp.mean(x * x, axis=-1, keepdims=True) + EPS) * g


def _sigmoid(x):
    return 0.5 * jnp.tanh(0.5 * x) + 0.5


def _dot(a, b):
    return jnp.dot(a, b, preferred_element_type=F32)


def _dot_nt(a, b):
    return lax.dot_general(a, b, (((1,), (1,)), ((), ())), preferred_element_type=F32)


def _in_proj_kernel(x_ref, g_ref, w_ref, wf_ref, cs_ref, cb_ref, z_ref, f_ref, xn_ref, *, gate_tile):
    j = pl.program_id(1)

    @pl.when(j == 0)
    def _():
        xn = _rms_norm(x_ref[...], g_ref[...]).astype(BF16)
        xn_ref[...] = xn
        f_ref[...] = _dot(xn, wf_ref[...])

    @pl.when(j < gate_tile)
    def _():
        z_ref[...] = (_dot(xn_ref[...], w_ref[...]) * cs_ref[...]).astype(z_ref.dtype)

    @pl.when(j >= gate_tile)
    def _():
        z_ref[...] = _sigmoid(_dot(xn_ref[...], w_ref[...]) + cb_ref[...]).astype(z_ref.dtype)


def _in_proj(x, gain, w_main, w_f, col_scale, col_bias, layer, *, tm, tn, gate_col):
    n, d = x.shape
    e = w_main.shape[-1]
    return pl.pallas_call(
        functools.partial(_in_proj_kernel, gate_tile=gate_col // tn),
        grid=(n // tm, e // tn),
        in_specs=[
            pl.BlockSpec((tm, d), lambda i, j: (i, 0)),
            pl.BlockSpec((None, 1, d), lambda i, j: (layer, 0, 0)),
            pl.BlockSpec((None, d, tn), lambda i, j: (layer, 0, j)),
            pl.BlockSpec((None, d, LANES), lambda i, j: (layer, 0, 0)),
            pl.BlockSpec((1, tn), lambda i, j: (0, j)),
            pl.BlockSpec((None, 1, tn), lambda i, j: (layer, 0, j)),
        ],
        out_specs=[
            pl.BlockSpec((tm, tn), lambda i, j: (i, j)),
            pl.BlockSpec((tm, LANES), lambda i, j: (i, 0)),
        ],
        out_shape=[
            jax.ShapeDtypeStruct((n, e), BF16),
            jax.ShapeDtypeStruct((n, LANES), F32),
        ],
        scratch_shapes=[pltpu.VMEM((tm, d), BF16)],
        compiler_params=_params("parallel", "arbitrary"),
        name="in_proj",
    )(x, gain, w_main, w_f, col_scale, col_bias)


def _forget_cumsum_kernel(f_ref, b_ref, tri_ref, crow_ref, c_scr):
    x = f_ref[...] + b_ref[...]
    lf = jnp.minimum(x, 0.0) - jnp.log(1.0 + jnp.exp(-jnp.abs(x)))
    hi = lf.astype(BF16)
    rem = lf - hi.astype(F32)
    mid = rem.astype(BF16)
    lo = (rem - mid.astype(F32)).astype(BF16)
    tri = tri_ref[...]
    seq = f_ref.shape[0]
    carry = jnp.zeros((1, LANES), F32)
    for blk in range(seq // CUM_BLOCK):
        sl = slice(blk * CUM_BLOCK, (blk + 1) * CUM_BLOCK)
        part = _dot(tri, hi[sl]) + _dot(tri, mid[sl]) + _dot(tri, lo[sl]) + carry
        c_scr[sl, :] = part
        carry = part[CUM_BLOCK - 1:CUM_BLOCK, :]
    crow_ref[...] = c_scr[...].T[:FOX_HEADS, :]


def _forget_cumsum(f, b_forget, layer, tri, *, batch, seq):
    return pl.pallas_call(
        _forget_cumsum_kernel,
        grid=(batch,),
        in_specs=[
            pl.BlockSpec((seq, LANES), lambda b: (b, 0)),
            pl.BlockSpec((None, 1, LANES), lambda b: (layer, 0, 0)),
            pl.BlockSpec((CUM_BLOCK, CUM_BLOCK), lambda b: (0, 0)),
        ],
        out_specs=pl.BlockSpec((None, FOX_HEADS, seq), lambda b: (b, 0, 0)),
        out_shape=jax.ShapeDtypeStruct((batch, FOX_HEADS, seq), F32),
        scratch_shapes=[pltpu.VMEM((seq, LANES), F32)],
        compiler_params=_params("parallel"),
        name="forget_cumsum",
    )(f, b_forget, tri)


def _fox_kernel(q_ref, k_ref, v_ref, crow_ref, o_ref, *, blk):
    lane = lax.broadcasted_iota(jnp.int32, (1, LANES), 1)
    seq = k_ref.shape[0]

    def online(state, s, vblk):
        m, l, acc = state
        m_new = jnp.maximum(m, jnp.max(s, axis=-1, keepdims=True))
        alpha = jnp.exp2(m - m_new)
        p = jnp.exp2(s - m_new)
        l = alpha * l + jnp.sum(p, axis=-1, keepdims=True)
        return m_new, l, alpha * acc + _dot(p.astype(BF16), vblk)

    for i in range(seq // blk):
        q = q_ref[i * blk:(i + 1) * blk, :]
        zero = jnp.zeros_like(q)
        qs = (jnp.where(lane < FOX_HEAD_DIM, q, zero), jnp.where(lane < FOX_HEAD_DIM, zero, q))
        init = (jnp.full((blk, 1), -jnp.inf, F32), jnp.zeros((blk, 1), F32), jnp.zeros((blk, LANES), F32))
        state = [init, init]
        for j in range(i + 1):
            keys = slice(j * blk, (j + 1) * blk)
            for h in range(2):
                s = _dot_nt(qs[h], k_ref[keys, :]) - crow_ref[h:h + 1, keys] * LOG2E
                if j == i:
                    row = lax.broadcasted_iota(jnp.int32, (blk, blk), 0)
                    col = lax.broadcasted_iota(jnp.int32, (blk, blk), 1)
                    s = jnp.where(row >= col, s, -jnp.inf)
                state[h] = online(state[h], s, v_ref[keys, :])
        outs = [state[h][2] / state[h][1] for h in range(2)]
        o_ref[i * blk:(i + 1) * blk, :] = jnp.where(lane < FOX_HEAD_DIM, outs[0], outs[1]).astype(o_ref.dtype)


def _fox_attention(z, c_row, *, batch, seq, q_col, k_col, v_col, blk):
    n = z.shape[0]
    pairs = FOX_HEADS // 2
    qb, kb, vb = q_col // LANES, k_col // LANES, v_col // LANES

    def cols(first):
        return pl.BlockSpec((seq, LANES), lambda b, p: (b, first + p))

    return pl.pallas_call(
        functools.partial(_fox_kernel, blk=blk),
        grid=(batch, pairs),
        in_specs=[cols(qb), cols(kb), cols(vb), pl.BlockSpec((None, None, 2, seq), lambda b, p: (b, p, 0, 0))],
        out_specs=cols(0),
        out_shape=jax.ShapeDtypeStruct((n, pairs * LANES), BF16),
        compiler_params=_params("parallel", "parallel"),
        name="fox_attention",
    )(z, z, z, c_row)


def _shift_rows(x3, r):
    sub = lax.broadcasted_iota(jnp.int32, x3.shape[1:], 0)[None]
    rolled = pltpu.roll(x3, SUBLANES - r, axis=1)
    return jnp.where(sub < SUBLANES - r, rolled[:-1], rolled[1:])


def _mixer_kernel(x_ref, ab_ref, ac_ref, au_ref, cv_ref, cg_ref, ga_ref, gb_ref, gc_ref,
                  hac_ref, hau_ref, hcv_ref, hcg_ref, o_ref,
                  conva_ref, convc_ref, cbias_ref, lng_ref, lnb_ref,
                  woa_ref, wob_ref, woc_ref, wo_ref, out_ref,
                  ush_scr, p_scr, uc_scr, pa_scr, wc_scr, wa_scr, *, tm, seq, rows):
    d = x_ref.shape[-1]
    hg = HALO // SUBLANES
    gu = hg + tm // SUBLANES
    gc = rows // SUBLANES

    def grouped(v):
        return v.reshape(v.shape[0] // SUBLANES, SUBLANES, d)

    for k in range(CONV_C_WIDTH):
        wc_scr[k] = jnp.broadcast_to(convc_ref[k:k + 1, :], (SUBLANES, d))
    for k in range(CONV_A_WIDTH):
        wa_scr[k] = jnp.broadcast_to(conva_ref[k:k + 1, :], (SUBLANES, d))

    keep = jnp.where((pl.program_id(0) * tm) % seq == 0, 0.0, 1.0)
    ush_scr[0, 0:hg] = grouped(hcv_ref[...].astype(F32) * hcg_ref[...].astype(F32) * keep)
    ush_scr[0, hg:] = grouped(cv_ref[...].astype(F32) * cg_ref[...].astype(F32))
    p_scr[0:hg] = grouped(hac_ref[...].astype(F32) * hau_ref[...].astype(F32) * keep)
    p_scr[hg:] = grouped(ac_ref[...].astype(F32) * au_ref[...].astype(F32))
    u0 = ush_scr[0]
    for r in range(1, SUBLANES):
        ush_scr[r, 0:gu - 1] = _shift_rows(u0, r)

    for c in range(tm // rows):
        g0 = c * gc
        r0 = c * rows
        acc = jnp.broadcast_to(cbias_ref[...][None], (gc, SUBLANES, d))
        for k in range(CONV_C_WIDTH):
            a, r = divmod(HALO - (CONV_C_WIDTH - 1) + k, SUBLANES)
            acc = acc + wc_scr[k][None] * ush_scr[r, g0 + a:g0 + a + gc]
        mu = jnp.mean(acc, axis=-1, keepdims=True)
        xc = acc - mu
        y = xc * lax.rsqrt(jnp.mean(xc * xc, axis=-1, keepdims=True) + EPS) * lng_ref[...][None] + lnb_ref[...][None]
        uc_scr[r0:r0 + rows, :] = (y * _sigmoid(y)).reshape(rows, d).astype(BF16)
        acc = None
        for k in range(CONV_A_WIDTH):
            a, r = divmod(HALO - (CONV_A_WIDTH - 1) + k, SUBLANES)
            if r == 0:
                tap = p_scr[g0 + a:g0 + a + gc]
            else:
                tap = _shift_rows(p_scr[g0 + a:g0 + a + gc + 1], r)
            term = wa_scr[k][None] * tap
            acc = term if acc is None else acc + term
        pa_scr[r0:r0 + rows, :] = (ab_ref[r0:r0 + rows, :].astype(F32) * acc.reshape(rows, d)).astype(BF16)

    merged = ga_ref[...].astype(F32) * _dot(pa_scr[...], woa_ref[...])
    merged = merged + gb_ref[...].astype(F32) * _dot(o_ref[...], wob_ref[...])
    merged = merged + gc_ref[...].astype(F32) * _dot(uc_scr[...], woc_ref[...])
    out_ref[...] = x_ref[...] + _dot(merged.astype(BF16), wo_ref[...])


def _mixer(x, z, o, conv_a, conv_c, conv_c_bias, ln_g, ln_b, w_out_a, w_out_b, w_out_c, w_o,
           layer, cols, *, seq, tm, rows=32):
    n, d = x.shape
    hb = tm // HALO
    cb = {name: c // d for name, c in cols.items()}

    def zspec(name):
        j = cb[name]
        return pl.BlockSpec((tm, d), lambda i: (i, j))

    def hspec(name):
        j = cb[name]
        return pl.BlockSpec((HALO, d), lambda i: (jnp.maximum(i * hb - 1, 0), j))

    def vec(k):
        return pl.BlockSpec((None, k, d), lambda i: (layer, 0, 0))

    wspec = pl.BlockSpec((None, d, d), lambda i: (layer, 0, 0))
    tile = pl.BlockSpec((tm, d), lambda i: (i, 0))
    return pl.pallas_call(
        functools.partial(_mixer_kernel, tm=tm, seq=seq, rows=rows),
        grid=(n // tm,),
        in_specs=[tile] + [zspec(k) for k in ("a_b", "a_c", "a_u", "c_val", "c_gate", "g_a", "g_b", "g_c")]
        + [hspec(k) for k in ("a_c", "a_u", "c_val", "c_gate")] + [tile]
        + [vec(CONV_A_WIDTH), vec(CONV_C_WIDTH), vec(1), vec(1), vec(1)]
        + [wspec] * 4,
        out_specs=tile,
        out_shape=jax.ShapeDtypeStruct((n, d), F32),
        scratch_shapes=[
            pltpu.VMEM((SUBLANES, (tm + HALO) // SUBLANES, SUBLANES, d), F32),
            pltpu.VMEM(((tm + HALO) // SUBLANES, SUBLANES, d), F32),
            pltpu.VMEM((tm, d), BF16),
            pltpu.VMEM((tm, d), BF16),
            pltpu.VMEM((CONV_C_WIDTH, SUBLANES, d), F32),
            pltpu.VMEM((CONV_A_WIDTH, SUBLANES, d), F32),
        ],
        compiler_params=_params("parallel"),
        name="mixer_merge",
    )(x, *([z] * 12), o, conv_a, conv_c, conv_c_bias, ln_g, ln_b, w_out_a, w_out_b, w_out_c, w_o)


def _mem_kv_kernel(mem_ref, g_ref, w_ref, kv_ref):
    kv_ref[...] = _dot(_rms_norm(mem_ref[...], g_ref[...]).astype(BF16), w_ref[...]).astype(kv_ref.dtype)


def _mem_kv(mem, gain, w_xkv):
    m, d = mem.shape
    depth, _, e = w_xkv.shape
    return pl.pallas_call(
        _mem_kv_kernel,
        grid=(depth,),
        in_specs=[
            pl.BlockSpec((m, d), lambda l: (0, 0)),
            pl.BlockSpec((1, d), lambda l: (0, 0)),
            pl.BlockSpec((None, d, e), lambda l: (l, 0, 0)),
        ],
        out_specs=pl.BlockSpec((None, m, e), lambda l: (l, 0, 0)),
        out_shape=jax.ShapeDtypeStruct((depth, m, e), BF16),
        compiler_params=_params("parallel"),
        name="mem_kv",
    )(mem, gain, w_xkv)


def _xattn_kernel(x_ref, g_ref, wq_ref, kv_ref, wo_ref, out_ref):
    x = x_ref[...]
    q = _dot(_rms_norm(x, g_ref[...]).astype(BF16), wq_ref[...]) * (X_HEAD_DIM ** -0.5)
    q = q.astype(BF16)
    dx = X_HEADS * X_HEAD_DIM
    outs = []
    for h in range(X_HEADS):
        sl = slice(h * X_HEAD_DIM, (h + 1) * X_HEAD_DIM)
        s = _dot_nt(q[:, sl], kv_ref[:, sl])
        p = jnp.exp(s - jnp.max(s, axis=-1, keepdims=True))
        oh = _dot(p.astype(BF16), kv_ref[:, dx + h * X_HEAD_DIM:dx + (h + 1) * X_HEAD_DIM])
        outs.append((oh / jnp.sum(p, axis=-1, keepdims=True)).astype(BF16))
    out_ref[...] = x + _dot(jnp.concatenate(outs, axis=-1), wo_ref[...])


def _xattn(x, gain, w_xq, kv, w_xo, layer, *, seq, n_mem, tm):
    n, d = x.shape
    dx = w_xq.shape[-1]
    per_seq = seq // tm
    tile = pl.BlockSpec((tm, d), lambda i: (i, 0))
    return pl.pallas_call(
        _xattn_kernel,
        grid=(n // tm,),
        in_specs=[
            tile,
            pl.BlockSpec((None, 1, d), lambda i: (layer, 0, 0)),
            pl.BlockSpec((None, d, dx), lambda i: (layer, 0, 0)),
            pl.BlockSpec((None, None, n_mem, 2 * dx), lambda i: (layer, i // per_seq, 0, 0)),
            pl.BlockSpec((None, dx, d), lambda i: (layer, 0, 0)),
        ],
        out_specs=tile,
        out_shape=jax.ShapeDtypeStruct((n, d), F32),
        compiler_params=_params("parallel"),
        name="xattn",
    )(x, gain, w_xq, kv, w_xo)


def _ffn_kernel(x_ref, g_ref, wg_ref, wu_ref, wd_ref, fg_ref, out_ref, *, final, chunks):
    x = x_ref[...]
    xn = _rms_norm(x, g_ref[...]).astype(BF16)
    y = x
    for c0, c1 in chunks:
        g = _dot(xn, wg_ref[:, c0:c1])
        h = (g * _sigmoid(g) * _dot(xn, wu_ref[:, c0:c1])).astype(BF16)
        y = y + _dot(h, wd_ref[c0:c1, :])
    if final:
        y = _rms_norm(y, fg_ref[...])
    out_ref[...] = y


def _ffn_chunks(dff):
    tiles = dff // MXU_WIDTH
    assert tiles * MXU_WIDTH == dff
    half = (tiles + 1) // 2
    return tuple((a * MXU_WIDTH, b * MXU_WIDTH) for a, b in ((0, half), (half, tiles)) if b > a)


def _ffn(x, gain, w_gate_up, w_down, final_gain, layer, *, tm, final):
    n, d = x.shape
    dff = w_down.shape[1]
    tile = pl.BlockSpec((tm, d), lambda i: (i, 0))
    once = pl.Buffered(1)
    return pl.pallas_call(
        functools.partial(_ffn_kernel, final=final, chunks=_ffn_chunks(dff)),
        grid=(n // tm,),
        in_specs=[
            tile,
            pl.BlockSpec((None, 1, d), lambda i: (layer, 0, 0)),
            pl.BlockSpec((None, d, dff), lambda i: (layer, 0, 0), pipeline_mode=once),
            pl.BlockSpec((None, d, dff), lambda i: (layer, 0, 1), pipeline_mode=once),
            pl.BlockSpec((None, dff, d), lambda i: (layer, 0, 0), pipeline_mode=once),
            pl.BlockSpec((1, d), lambda i: (0, 0)),
        ],
        out_specs=tile,
        out_shape=jax.ShapeDtypeStruct((n, d), F32),
        compiler_params=_params("parallel"),
        name="ffn",
    )(x, gain, w_gate_up, w_gate_up, w_down, final_gain)


def _pick(n, pref):
    t = min(n, pref)
    assert n % t == 0, (n, t)
    return t


def kernel(x, mem, mix_norm, w_in, b_gate, b_forget, conv_a, w_out_a, w_out_b, conv_c, conv_c_bias,
           ln_c_gain, ln_c_bias, w_out_c, w_o, xattn_norm, mem_norm, w_xq, w_xkv, w_xo, ffn_norm,
           w_gate_up, w_down, final_norm):
    batch, seq, d = x.shape
    depth = w_in.shape[0]
    n_mem = mem.shape[1]
    n = batch * seq
    d_b = FOX_HEADS * FOX_HEAD_DIM
    assert d % LANES == 0 and seq % CUM_BLOCK == 0 and d_b == d

    names = ("a_b", "a_c", "a_u", "q", "k", "v", "c_val", "c_gate", "g_a", "g_b", "g_c")
    cols = {name: idx * d for idx, name in enumerate(names)}
    f_col = 6 * d
    w_in16 = w_in.astype(BF16)
    w_main = jnp.concatenate([w_in16[:, :, :f_col], w_in16[:, :, f_col + FOX_HEADS:]], axis=-1)
    w_f = jnp.pad(w_in16[:, :, f_col:f_col + FOX_HEADS], ((0, 0), (0, 0), (0, LANES - FOX_HEADS)))
    b_f = jnp.pad(b_forget, ((0, 0), (0, LANES - FOX_HEADS)))[:, None, :]
    tri = jnp.tril(jnp.ones((CUM_BLOCK, CUM_BLOCK), BF16))
    col_scale = jnp.ones((1, w_main.shape[-1]), F32).at[:, cols["q"]:cols["q"] + d_b].set(
        FOX_HEAD_DIM ** -0.5 * LOG2E)

    row3 = lambda a: a[:, None, :]
    bf = lambda a: a.astype(BF16)
    gate_col = cols["c_gate"]
    col_bias = jnp.zeros((depth, 1, w_main.shape[-1]), F32).at[:, 0, cols["g_a"]:].set(b_gate)

    kv = _mem_kv(mem.reshape(batch * n_mem, d), mem_norm[None, :], bf(w_xkv))
    kv = kv.reshape(depth, batch, n_mem, kv.shape[-1])
    w_oa, w_ob, w_oc, w_oo = bf(w_out_a), bf(w_out_b), bf(w_out_c), bf(w_o)
    w_q, w_xo_b, w_gu, w_dn = bf(w_xq), bf(w_xo), bf(w_gate_up), bf(w_down)

    xs = x.reshape(n, d)
    for l in range(depth):
        z, f = _in_proj(xs, row3(mix_norm), w_main, w_f, col_scale, col_bias, l, tm=_pick(n, 2048),
                        tn=_pick(d, 1024), gate_col=gate_col)
        c_row = _forget_cumsum(f, b_f, l, tri, batch=batch, seq=seq)
        c_row = c_row.reshape(batch, FOX_HEADS // 2, 2, seq)
        o = _fox_attention(z, c_row, batch=batch, seq=seq, q_col=cols["q"], k_col=cols["k"],
                           v_col=cols["v"], blk=_pick(seq, 512))
        xs = _mixer(xs, z, o, conv_a, conv_c, row3(conv_c_bias), row3(ln_c_gain), row3(ln_c_bias),
                    w_oa, w_ob, w_oc, w_oo, l, cols, seq=seq, tm=_pick(seq, 256))
        xs = _xattn(xs, row3(xattn_norm), w_q, kv, w_xo_b, l, seq=seq, n_mem=n_mem, tm=_pick(seq, 512))
        xs = _ffn(xs, row3(ffn_norm), w_gu, w_dn, final_norm[None, :], l, tm=_pick(n, 512),
                  final=(l == depth - 1))
    return xs.reshape(batch, seq, d)
```

```python
import functools

import jax
import jax.numpy as jnp
from jax import lax
from jax.experimental import pallas as pl
from jax.experimental.pallas import tpu as pltpu

EPS = 1e-6
FOX_HEADS = 16
FOX_HEAD_DIM = 64
X_HEADS = 4
X_HEAD_DIM = 128
CONV_A_WIDTH = 3
CONV_C_WIDTH = 31
LANES = 128
HALO = 32
MXU_WIDTH = 256
CUM_BLOCK = MXU_WIDTH
VMEM_LIMIT = 56 * 1024 * 1024
SUBLANES = 8
LOG2E = 1.4426950408889634

F32 = jnp.float32
BF16 = jnp.bfloat16


def _params(*sem):
    return pltpu.CompilerParams(dimension_semantics=sem, vmem_limit_bytes=VMEM_LIMIT)


def _rms_norm(x, g):
    return x * lax.rsqrt(jnp.mean(x * x, axis=-1, keepdims=True) + EPS) * g


def _sigmoid(x):
    return 0.5 * jnp.tanh(0.5 * x) + 0.5


def _dot(a, b):
    return jnp.dot(a, b, preferred_element_type=F32)


def _dot_nt(a, b):
    return lax.dot_general(a, b, (((1,), (1,)), ((), ())), preferred_element_type=F32)


def _in_proj_kernel(xn_ref, wlo_ref, whi_ref, wf_ref, cs_ref, cb_ref, z_ref, f_ref, *, lo_tiles, gate_tile):
    j = pl.program_id(1)

    @pl.when(j == 0)
    def _():
        f_ref[...] = _dot(xn_ref[...], wf_ref[...])

    @pl.when(j < lo_tiles)
    def _():
        z_ref[...] = (_dot(xn_ref[...], wlo_ref[...]) * cs_ref[...]).astype(z_ref.dtype)

    @pl.when((j >= lo_tiles) & (j < gate_tile))
    def _():
        z_ref[...] = (_dot(xn_ref[...], whi_ref[...]) * cs_ref[...]).astype(z_ref.dtype)

    @pl.when(j >= gate_tile)
    def _():
        z_ref[...] = _sigmoid(_dot(xn_ref[...], whi_ref[...]) + cb_ref[...]).astype(z_ref.dtype)


def _in_proj(xn, w_lo, w_hi, w_f, col_scale, col_bias, layer, *, tm, tn, lo_cols, gate_col):
    n, d = xn.shape
    e = lo_cols + w_hi.shape[-1]
    lo_tiles = lo_cols // tn
    assert lo_tiles * tn == lo_cols and gate_col % tn == 0
    return pl.pallas_call(
        functools.partial(_in_proj_kernel, lo_tiles=lo_tiles, gate_tile=gate_col // tn),
        grid=(n // tm, e // tn),
        in_specs=[
            pl.BlockSpec((tm, d), lambda i, j: (i, 0)),
            pl.BlockSpec((None, d, tn), lambda i, j: (layer, 0, jnp.minimum(j, lo_tiles - 1))),
            pl.BlockSpec((None, d, tn), lambda i, j: (layer, 0, jnp.maximum(j - lo_tiles, 0))),
            pl.BlockSpec((None, d, LANES), lambda i, j: (layer, 0, 0)),
            pl.BlockSpec((1, tn), lambda i, j: (0, j)),
            pl.BlockSpec((None, 1, tn), lambda i, j: (layer, 0, j)),
        ],
        out_specs=[
            pl.BlockSpec((tm, tn), lambda i, j: (i, j)),
            pl.BlockSpec((tm, LANES), lambda i, j: (i, 0)),
        ],
        out_shape=[
            jax.ShapeDtypeStruct((n, e), BF16),
            jax.ShapeDtypeStruct((n, LANES), F32),
        ],
        compiler_params=_params("parallel", "arbitrary"),
        name="in_proj",
    )(xn, w_lo, w_hi, w_f, col_scale, col_bias)


def _norm_cast_kernel(x_ref, g_ref, xn_ref):
    xn_ref[...] = _rms_norm(x_ref[...], g_ref[...]).astype(xn_ref.dtype)


def _norm_cast(x, gain, layer, *, tm):
    n, d = x.shape
    tile = pl.BlockSpec((tm, d), lambda i: (i, 0))
    return pl.pallas_call(
        _norm_cast_kernel,
        grid=(n // tm,),
        in_specs=[tile, pl.BlockSpec((None, 1, d), lambda i: (layer, 0, 0))],
        out_specs=tile,
        out_shape=jax.ShapeDtypeStruct((n, d), BF16),
        compiler_params=_params("parallel"),
        name="norm_cast",
    )(x, gain)


def _forget_cumsum_kernel(f_ref, b_ref, tri_ref, crow_ref, c_scr):
    x = f_ref[...] + b_ref[...]
    lf = jnp.minimum(x, 0.0) - jnp.log(1.0 + jnp.exp(-jnp.abs(x)))
    hi = lf.astype(BF16)
    rem = lf - hi.astype(F32)
    mid = rem.astype(BF16)
    lo = (rem - mid.astype(F32)).astype(BF16)
    tri = tri_ref[...]
    seq = f_ref.shape[0]
    carry = jnp.zeros((1, LANES), F32)
    for blk in range(seq // CUM_BLOCK):
        sl = slice(blk * CUM_BLOCK, (blk + 1) * CUM_BLOCK)
        part = _dot(tri, hi[sl]) + _dot(tri, mid[sl]) + _dot(tri, lo[sl]) + carry
        c_scr[sl, :] = part
        carry = part[CUM_BLOCK - 1:CUM_BLOCK, :]
    crow_ref[...] = c_scr[...].T[:FOX_HEADS, :]


def _forget_cumsum(f, b_forget, layer, tri, *, batch, seq):
    return pl.pallas_call(
        _forget_cumsum_kernel,
        grid=(batch,),
        in_specs=[
            pl.BlockSpec((seq, LANES), lambda b: (b, 0)),
            pl.BlockSpec((None, 1, LANES), lambda b: (layer, 0, 0)),
            pl.BlockSpec((CUM_BLOCK, CUM_BLOCK), lambda b: (0, 0)),
        ],
        out_specs=pl.BlockSpec((None, FOX_HEADS, seq), lambda b: (b, 0, 0)),
        out_shape=jax.ShapeDtypeStruct((batch, FOX_HEADS, seq), F32),
        scratch_shapes=[pltpu.VMEM((seq, LANES), F32)],
        compiler_params=_params("parallel"),
        name="forget_cumsum",
    )(f, b_forget, tri)


def _fox_kernel(q_ref, k_ref, v_ref, crow_ref, o_ref, *, blk):
    lane = lax.broadcasted_iota(jnp.int32, (1, LANES), 1)
    seq = k_ref.shape[0]

    def online(state, s, vblk):
        m, acc = state
        m_new = jnp.maximum(m, jnp.max(s, axis=-1, keepdims=True))
        p = jnp.exp2(s - m_new)
        return m_new, jnp.exp2(m - m_new) * acc + _dot(p.astype(BF16), vblk)

    for i in range(seq // blk):
        q = q_ref[i * blk:(i + 1) * blk, :]
        zero = jnp.zeros_like(q)
        qs = (jnp.where(lane < FOX_HEAD_DIM, q, zero), jnp.where(lane < FOX_HEAD_DIM, zero, q))
        init = (jnp.full((blk, 1), -jnp.inf, F32), jnp.zeros((blk, LANES), F32))
        state = [init, init]
        for j in range(i + 1):
            keys = slice(j * blk, (j + 1) * blk)
            v = v_ref[keys, :]
            ones = jnp.ones_like(v)
            vs = (jnp.where(lane < FOX_HEAD_DIM, v, ones), jnp.where(lane < FOX_HEAD_DIM, ones, v))
            for h in range(2):
                s = _dot_nt(qs[h], k_ref[keys, :]) - crow_ref[h:h + 1, keys] * LOG2E
                if j == i:
                    row = lax.broadcasted_iota(jnp.int32, (blk, blk), 0)
                    col = lax.broadcasted_iota(jnp.int32, (blk, blk), 1)
                    s = jnp.where(row >= col, s, -jnp.inf)
                state[h] = online(state[h], s, vs[h])
        acc0, acc1 = state[0][1], state[1][1]
        num = jnp.where(lane < FOX_HEAD_DIM, acc0, acc1)
        den = jnp.where(lane < FOX_HEAD_DIM, pltpu.roll(acc0, FOX_HEAD_DIM, axis=1), pltpu.roll(acc1, FOX_HEAD_DIM, axis=1))
        o_ref[i * blk:(i + 1) * blk, :] = (num / den).astype(o_ref.dtype)


def _fox_attention(z, c_row, *, batch, seq, q_col, k_col, v_col, blk):
    n = z.shape[0]
    pairs = FOX_HEADS // 2
    qb, kb, vb = q_col // LANES, k_col // LANES, v_col // LANES

    def cols(first):
        return pl.BlockSpec((seq, LANES), lambda b, p: (b, first + p))

    return pl.pallas_call(
        functools.partial(_fox_kernel, blk=blk),
        grid=(batch, pairs),
        in_specs=[cols(qb), cols(kb), cols(vb), pl.BlockSpec((None, None, 2, seq), lambda b, p: (b, p, 0, 0))],
        out_specs=cols(0),
        out_shape=jax.ShapeDtypeStruct((n, pairs * LANES), BF16),
        compiler_params=_params("parallel", "parallel"),
        name="fox_attention",
    )(z, z, z, c_row)


def _shift_rows(x3, r):
    sub = lax.broadcasted_iota(jnp.int32, x3.shape[1:], 0)[None]
    rolled = pltpu.roll(x3, SUBLANES - r, axis=1)
    return jnp.where(sub < SUBLANES - r, rolled[:-1], rolled[1:])


def _mixer_kernel(x_ref, ab_ref, ac_ref, au_ref, cv_ref, cg_ref, ga_ref, gb_ref, gc_ref,
                  hac_ref, hau_ref, hcv_ref, hcg_ref, o_ref,
                  conva_ref, convc_ref, cbias_ref, lng_ref, lnb_ref,
                  woa_ref, wob_ref, woc_ref, wo_ref, out_ref,
                  ush_scr, p_scr, uc_scr, pa_scr, wc_scr, wa_scr, *, tm, seq, rows):
    d = x_ref.shape[-1]
    hg = HALO // SUBLANES
    gu = hg + tm // SUBLANES
    gc = rows // SUBLANES

    def grouped(v):
        return v.reshape(v.shape[0] // SUBLANES, SUBLANES, d)

    for k in range(CONV_C_WIDTH):
        wc_scr[k] = jnp.broadcast_to(convc_ref[k:k + 1, :], (SUBLANES, d))
    for k in range(CONV_A_WIDTH):
        wa_scr[k] = jnp.broadcast_to(conva_ref[k:k + 1, :], (SUBLANES, d))

    keep = jnp.where((pl.program_id(0) * tm) % seq == 0, 0.0, 1.0)
    ush_scr[0, 0:hg] = grouped(hcv_ref[...].astype(F32) * hcg_ref[...].astype(F32) * keep)
    ush_scr[0, hg:] = grouped(cv_ref[...].astype(F32) * cg_ref[...].astype(F32))
    p_scr[0:hg] = grouped(hac_ref[...].astype(F32) * hau_ref[...].astype(F32) * keep)
    p_scr[hg:] = grouped(ac_ref[...].astype(F32) * au_ref[...].astype(F32))
    u0 = ush_scr[0]
    for r in range(1, SUBLANES):
        ush_scr[r, 0:gu - 1] = _shift_rows(u0, r)

    for c in range(tm // rows):
        g0 = c * gc
        r0 = c * rows
        acc = jnp.broadcast_to(cbias_ref[...][None], (gc, SUBLANES, d))
        for k in range(CONV_C_WIDTH):
            a, r = divmod(HALO - (CONV_C_WIDTH - 1) + k, SUBLANES)
            acc = acc + wc_scr[k][None] * ush_scr[r, g0 + a:g0 + a + gc]
        mu = jnp.mean(acc, axis=-1, keepdims=True)
        xc = acc - mu
        y = xc * lax.rsqrt(jnp.mean(xc * xc, axis=-1, keepdims=True) + EPS) * lng_ref[...][None] + lnb_ref[...][None]
        uc_scr[r0:r0 + rows, :] = (y * _sigmoid(y)).reshape(rows, d).astype(BF16)
        acc = None
        for k in range(CONV_A_WIDTH):
            a, r = divmod(HALO - (CONV_A_WIDTH - 1) + k, SUBLANES)
            if r == 0:
                tap = p_scr[g0 + a:g0 + a + gc]
            else:
                tap = _shift_rows(p_scr[g0 + a:g0 + a + gc + 1], r)
            term = wa_scr[k][None] * tap
            acc = term if acc is None else acc + term
        pa_scr[r0:r0 + rows, :] = (ab_ref[r0:r0 + rows, :].astype(F32) * acc.reshape(rows, d)).astype(BF16)

    merged = ga_ref[...].astype(F32) * _dot(pa_scr[...], woa_ref[...])
    merged = merged + gb_ref[...].astype(F32) * _dot(o_ref[...], wob_ref[...])
    merged = merged + gc_ref[...].astype(F32) * _dot(uc_scr[...], woc_ref[...])
    out_ref[...] = x_ref[...] + _dot(merged.astype(BF16), wo_ref[...])


def _mixer(x, z, o, conv_a, conv_c, conv_c_bias, ln_g, ln_b, w_out_a, w_out_b, w_out_c, w_o,
           layer, cols, *, seq, tm, rows=32):
    n, d = x.shape
    hb = tm // HALO
    cb = {name: c // d for name, c in cols.items()}

    def zspec(name):
        j = cb[name]
        return pl.BlockSpec((tm, d), lambda i: (i, j))

    def hspec(name):
        j = cb[name]
        return pl.BlockSpec((HALO, d), lambda i: (jnp.maximum(i * hb - 1, 0), j))

    def vec(k):
        return pl.BlockSpec((None, k, d), lambda i: (layer, 0, 0))

    wspec = pl.BlockSpec((None, d, d), lambda i: (layer, 0, 0))
    tile = pl.BlockSpec((tm, d), lambda i: (i, 0))
    return pl.pallas_call(
        functools.partial(_mixer_kernel, tm=tm, seq=seq, rows=rows),
        grid=(n // tm,),
        in_specs=[tile] + [zspec(k) for k in ("a_b", "a_c", "a_u", "c_val", "c_gate", "g_a", "g_b", "g_c")]
        + [hspec(k) for k in ("a_c", "a_u", "c_val", "c_gate")] + [tile]
        + [vec(CONV_A_WIDTH), vec(CONV_C_WIDTH), vec(1), vec(1), vec(1)]
        + [wspec] * 4,
        out_specs=tile,
        out_shape=jax.ShapeDtypeStruct((n, d), F32),
        scratch_shapes=[
            pltpu.VMEM((SUBLANES, (tm + HALO) // SUBLANES, SUBLANES, d), F32),
            pltpu.VMEM(((tm + HALO) // SUBLANES, SUBLANES, d), F32),
            pltpu.VMEM((tm, d), BF16),
            pltpu.VMEM((tm, d), BF16),
            pltpu.VMEM((CONV_C_WIDTH, SUBLANES, d), F32),
            pltpu.VMEM((CONV_A_WIDTH, SUBLANES, d), F32),
        ],
        compiler_params=_params("parallel"),
        name="mixer_merge",
    )(x, *([z] * 12), o, conv_a, conv_c, conv_c_bias, ln_g, ln_b, w_out_a, w_out_b, w_out_c, w_o)


def _mem_kv_kernel(mem_ref, g_ref, w_ref, kv_ref):
    kv_ref[...] = _dot(_rms_norm(mem_ref[...], g_ref[...]).astype(BF16), w_ref[...]).astype(kv_ref.dtype)


def _mem_kv(mem, gain, w_xkv):
    m, d = mem.shape
    depth, _, e = w_xkv.shape
    return pl.pallas_call(
        _mem_kv_kernel,
        grid=(depth,),
        in_specs=[
            pl.BlockSpec((m, d), lambda l: (0, 0)),
            pl.BlockSpec((1, d), lambda l: (0, 0)),
            pl.BlockSpec((None, d, e), lambda l: (l, 0, 0)),
        ],
        out_specs=pl.BlockSpec((None, m, e), lambda l: (l, 0, 0)),
        out_shape=jax.ShapeDtypeStruct((depth, m, e), BF16),
        compiler_params=_params("parallel"),
        name="mem_kv",
    )(mem, gain, w_xkv)


def _xattn_kernel(x_ref, g_ref, wq_ref, kv_ref, wo_ref, out_ref):
    x = x_ref[...]
    q = _dot(_rms_norm(x, g_ref[...]).astype(BF16), wq_ref[...]) * (X_HEAD_DIM ** -0.5)
    q = q.astype(BF16)
    dx = X_HEADS * X_HEAD_DIM
    outs = []
    for h in range(X_HEADS):
        sl = slice(h * X_HEAD_DIM, (h + 1) * X_HEAD_DIM)
        s = _dot_nt(q[:, sl], kv_ref[:, sl])
        p = jnp.exp(s - jnp.max(s, axis=-1, keepdims=True))
        oh = _dot(p.astype(BF16), kv_ref[:, dx + h * X_HEAD_DIM:dx + (h + 1) * X_HEAD_DIM])
        outs.append((oh / jnp.sum(p, axis=-1, keepdims=True)).astype(BF16))
    out_ref[...] = x + _dot(jnp.concatenate(outs, axis=-1), wo_ref[...])


def _xattn(x, gain, w_xq, kv, w_xo, layer, *, seq, n_mem, tm):
    n, d = x.shape
    dx = w_xq.shape[-1]
    per_seq = seq // tm
    tile = pl.BlockSpec((tm, d), lambda i: (i, 0))
    return pl.pallas_call(
        _xattn_kernel,
        grid=(n // tm,),
        in_specs=[
            tile,
            pl.BlockSpec((None, 1, d), lambda i: (layer, 0, 0)),
            pl.BlockSpec((None, d, dx), lambda i: (layer, 0, 0)),
            pl.BlockSpec((None, None, n_mem, 2 * dx), lambda i: (layer, i // per_seq, 0, 0)),
            pl.BlockSpec((None, dx, d), lambda i: (layer, 0, 0)),
        ],
        out_specs=tile,
        out_shape=jax.ShapeDtypeStruct((n, d), F32),
        compiler_params=_params("parallel"),
        name="xattn",
    )(x, gain, w_xq, kv, w_xo)


def _ffn_kernel(x_ref, g_ref, wg_ref, wu_ref, wd_ref, ng_ref, *out_refs, final, chunks):
    x = x_ref[...]
    xn = _rms_norm(x, g_ref[...]).astype(BF16)
    y = x
    for c0, c1 in chunks:
        g = _dot(xn, wg_ref[:, c0:c1])
        h = (g * _sigmoid(g) * _dot(xn, wu_ref[:, c0:c1])).astype(BF16)
        y = y + _dot(h, wd_ref[c0:c1, :])
    yn = _rms_norm(y, ng_ref[...])
    if final:
        out_refs[0][...] = yn
    else:
        out_refs[0][...] = y
        out_refs[1][...] = yn.astype(out_refs[1].dtype)


def _ffn_chunks(dff):
    tiles = dff // MXU_WIDTH
    assert tiles * MXU_WIDTH == dff
    half = (tiles + 1) // 2
    return tuple((a * MXU_WIDTH, b * MXU_WIDTH) for a, b in ((0, half), (half, tiles)) if b > a)


def _ffn(x, gain, w_gate_up, w_down, next_gain, layer, *, tm, final):
    n, d = x.shape
    dff = w_down.shape[1]
    tile = pl.BlockSpec((tm, d), lambda i: (i, 0))
    once = pl.Buffered(1)
    return pl.pallas_call(
        functools.partial(_ffn_kernel, final=final, chunks=_ffn_chunks(dff)),
        grid=(n // tm,),
        in_specs=[
            tile,
            pl.BlockSpec((None, 1, d), lambda i: (layer, 0, 0)),
            pl.BlockSpec((None, d, dff), lambda i: (layer, 0, 0), pipeline_mode=once),
            pl.BlockSpec((None, d, dff), lambda i: (layer, 0, 1), pipeline_mode=once),
            pl.BlockSpec((None, dff, d), lambda i: (layer, 0, 0), pipeline_mode=once),
            pl.BlockSpec((1, d), lambda i: (0, 0)),
        ],
        out_specs=tile if final else [tile, tile],
        out_shape=(jax.ShapeDtypeStruct((n, d), F32) if final
                   else [jax.ShapeDtypeStruct((n, d), F32), jax.ShapeDtypeStruct((n, d), BF16)]),
        compiler_params=_params("parallel"),
        name="ffn",
    )(x, gain, w_gate_up, w_gate_up, w_down, next_gain)


def _pick(n, pref):
    t = min(n, pref)
    assert n % t == 0, (n, t)
    return t


def kernel(x, mem, mix_norm, w_in, b_gate, b_forget, conv_a, w_out_a, w_out_b, conv_c, conv_c_bias,
           ln_c_gain, ln_c_bias, w_out_c, w_o, xattn_norm, mem_norm, w_xq, w_xkv, w_xo, ffn_norm,
           w_gate_up, w_down, final_norm):
    batch, seq, d = x.shape
    depth = w_in.shape[0]
    n_mem = mem.shape[1]
    n = batch * seq
    d_b = FOX_HEADS * FOX_HEAD_DIM
    assert d % LANES == 0 and seq % CUM_BLOCK == 0 and d_b == d

    names = ("a_b", "a_c", "a_u", "q", "k", "v", "c_val", "c_gate", "g_a", "g_b", "g_c")
    cols = {name: idx * d for idx, name in enumerate(names)}
    f_col = 6 * d
    w_in16 = w_in.astype(BF16)
    w_hi = w_in16[:, :, f_col + FOX_HEADS:]
    e_main = f_col + w_hi.shape[-1]
    w_f = jnp.pad(w_in16[:, :, f_col:f_col + FOX_HEADS], ((0, 0), (0, 0), (0, LANES - FOX_HEADS)))
    b_f = jnp.pad(b_forget, ((0, 0), (0, LANES - FOX_HEADS)))[:, None, :]
    tri = jnp.tril(jnp.ones((CUM_BLOCK, CUM_BLOCK), BF16))
    col_scale = jnp.ones((1, e_main), F32).at[:, cols["q"]:cols["q"] + d_b].set(
        FOX_HEAD_DIM ** -0.5 * LOG2E)

    row3 = lambda a: a[:, None, :]
    bf = lambda a: a.astype(BF16)
    gate_col = cols["c_gate"]
    col_bias = jnp.zeros((depth, 1, e_main), F32).at[:, 0, cols["g_a"]:].set(b_gate)

    kv = _mem_kv(mem.reshape(batch * n_mem, d), mem_norm[None, :], bf(w_xkv))
    kv = kv.reshape(depth, batch, n_mem, kv.shape[-1])
    w_oa, w_ob, w_oc, w_oo = bf(w_out_a), bf(w_out_b), bf(w_out_c), bf(w_o)
    w_q, w_xo_b, w_gu, w_dn = bf(w_xq), bf(w_xo), bf(w_gate_up), bf(w_down)

    xs = x.reshape(n, d)
    xn = _norm_cast(xs, row3(mix_norm), 0, tm=_pick(n, 1024))
    for l in range(depth):
        z, f = _in_proj(xn, w_in16, w_hi, w_f, col_scale, col_bias, l, tm=_pick(n, 2048),
                        tn=_pick(d, 1024), lo_cols=f_col, gate_col=gate_col)
        c_row = _forget_cumsum(f, b_f, l, tri, batch=batch, seq=seq)
        c_row = c_row.reshape(batch, FOX_HEADS // 2, 2, seq)
        o = _fox_attention(z, c_row, batch=batch, seq=seq, q_col=cols["q"], k_col=cols["k"],
                           v_col=cols["v"], blk=_pick(seq, 512))
        xs = _mixer(xs, z, o, conv_a, conv_c, row3(conv_c_bias), row3(ln_c_gain), row3(ln_c_bias),
                    w_oa, w_ob, w_oc, w_oo, l, cols, seq=seq, tm=_pick(seq, 256))
        xs = _xattn(xs, row3(xattn_norm), w_q, kv, w_xo_b, l, seq=seq, n_mem=n_mem, tm=_pick(seq, 512))
        if l == depth - 1:
            xs = _ffn(xs, row3(ffn_norm), w_gu, w_dn, final_norm[None, :], l, tm=_pick(n, 512), final=True)
        else:
            xs, xn = _ffn(xs, row3(ffn_norm), w_gu, w_dn, mix_norm[l + 1][None, :], l, tm=_pick(n, 512),
                          final=False)
    return xs.reshape(batch, seq, d)
```

```python
import functools

import jax
import jax.numpy as jnp
from jax import lax
from jax.experimental import pallas as pl
from jax.experimental.pallas import tpu as pltpu

EPS = 1e-6
FOX_HEADS = 16
FOX_HEAD_DIM = 64
X_HEADS = 4
X_HEAD_DIM = 128
CONV_A_WIDTH = 3
CONV_C_WIDTH = 31
LANES = 128
HALO = 32
MXU_WIDTH = 256
CUM_BLOCK = MXU_WIDTH
VMEM_LIMIT = 56 * 1024 * 1024
SUBLANES = 8
LOG2E = 1.4426950408889634

F32 = jnp.float32
BF16 = jnp.bfloat16


def _params(*sem):
    return pltpu.CompilerParams(dimension_semantics=sem, vmem_limit_bytes=VMEM_LIMIT)


def _rms_norm(x, g):
    return x * lax.rsqrt(jnp.mean(x * x, axis=-1, keepdims=True) + EPS) * g


def _sigmoid(x):
    return 0.5 * jnp.tanh(0.5 * x) + 0.5


def _dot(a, b):
    return jnp.dot(a, b, preferred_element_type=F32)


def _dot_nt(a, b):
    return lax.dot_general(a, b, (((1,), (1,)), ((), ())), preferred_element_type=F32)


def _in_proj_kernel(x_ref, g_ref, w_ref, wf_ref, cs_ref, cb_ref, z_ref, f_ref, xn_ref, *, gate_tile):
    j = pl.program_id(1)

    @pl.when(j == 0)
    def _():
        xn = _rms_norm(x_ref[...], g_ref[...]).astype(BF16)
        xn_ref[...] = xn
        f_ref[...] = _dot(xn, wf_ref[...])

    @pl.when(j < gate_tile)
    def _():
        z_ref[...] = (_dot(xn_ref[...], w_ref[...]) * cs_ref[...]).astype(z_ref.dtype)

    @pl.when(j >= gate_tile)
    def _():
        z_ref[...] = _sigmoid(_dot(xn_ref[...], w_ref[...]) + cb_ref[...]).astype(z_ref.dtype)


def _in_proj(x, gain, w_main, w_f, col_scale, col_bias, layer, *, tm, tn, gate_col):
    n, d = x.shape
    e = w_main.shape[-1]
    return pl.pallas_call(
        functools.partial(_in_proj_kernel, gate_tile=gate_col // tn),
        grid=(n // tm, e // tn),
        in_specs=[
            pl.BlockSpec((tm, d), lambda i, j: (i, 0)),
            pl.BlockSpec((None, 1, d), lambda i, j: (layer, 0, 0)),
            pl.BlockSpec((None, d, tn), lambda i, j: (layer, 0, j)),
            pl.BlockSpec((None, d, LANES), lambda i, j: (layer, 0, 0)),
            pl.BlockSpec((1, tn), lambda i, j: (0, j)),
            pl.BlockSpec((None, 1, tn), lambda i, j: (layer, 0, j)),
        ],
        out_specs=[
            pl.BlockSpec((tm, tn), lambda i, j: (i, j)),
            pl.BlockSpec((tm, LANES), lambda i, j: (i, 0)),
        ],
        out_shape=[
            jax.ShapeDtypeStruct((n, e), BF16),
            jax.ShapeDtypeStruct((n, LANES), F32),
        ],
        scratch_shapes=[pltpu.VMEM((tm, d), BF16)],
        compiler_params=_params("parallel", "arbitrary"),
        name="in_proj",
    )(x, gain, w_main, w_f, col_scale, col_bias)


def _forget_cumsum_kernel(f_ref, b_ref, tri_ref, crow_ref, c_scr):
    x = f_ref[...] + b_ref[...]
    lf = jnp.minimum(x, 0.0) - jnp.log(1.0 + jnp.exp(-jnp.abs(x)))
    hi = lf.astype(BF16)
    rem = lf - hi.astype(F32)
    mid = rem.astype(BF16)
    lo = (rem - mid.astype(F32)).astype(BF16)
    tri = tri_ref[...]
    seq = f_ref.shape[0]
    carry = jnp.zeros((1, LANES), F32)
    for blk in range(seq // CUM_BLOCK):
        sl = slice(blk * CUM_BLOCK, (blk + 1) * CUM_BLOCK)
        part = _dot(tri, hi[sl]) + _dot(tri, mid[sl]) + _dot(tri, lo[sl]) + carry
        c_scr[sl, :] = part
        carry = part[CUM_BLOCK - 1:CUM_BLOCK, :]
    crow_ref[...] = c_scr[...].T[:FOX_HEADS, :]


def _forget_cumsum(f, b_forget, layer, tri, *, batch, seq):
    return pl.pallas_call(
        _forget_cumsum_kernel,
        grid=(batch,),
        in_specs=[
            pl.BlockSpec((seq, LANES), lambda b: (b, 0)),
            pl.BlockSpec((None, 1, LANES), lambda b: (layer, 0, 0)),
            pl.BlockSpec((CUM_BLOCK, CUM_BLOCK), lambda b: (0, 0)),
        ],
        out_specs=pl.BlockSpec((None, FOX_HEADS, seq), lambda b: (b, 0, 0)),
        out_shape=jax.ShapeDtypeStruct((batch, FOX_HEADS, seq), F32),
        scratch_shapes=[pltpu.VMEM((seq, LANES), F32)],
        compiler_params=_params("parallel"),
        name="forget_cumsum",
    )(f, b_forget, tri)


def _fox_kernel(q_ref, k_ref, v_ref, crow_ref, o_ref, *, blk):
    lane = lax.broadcasted_iota(jnp.int32, (1, LANES), 1)
    seq = k_ref.shape[0]

    def online(state, s, vblk):
        m, acc = state
        m_new = jnp.maximum(m, jnp.max(s, axis=-1, keepdims=True))
        p = jnp.exp2(s - m_new)
        return m_new, jnp.exp2(m - m_new) * acc + _dot(p.astype(BF16), vblk)

    def head_pair(pair, carry):
        lanes = pl.ds(pl.multiple_of(pair * LANES, LANES), LANES)
        for i in range(seq // blk):
            pair_block(pair, lanes, i)
        return carry

    def pair_block(pair, lanes, i):
        q = q_ref[i * blk:(i + 1) * blk, lanes]
        zero = jnp.zeros_like(q)
        qs = (jnp.where(lane < FOX_HEAD_DIM, q, zero), jnp.where(lane < FOX_HEAD_DIM, zero, q))
        init = (jnp.full((blk, 1), -jnp.inf, F32), jnp.zeros((blk, LANES), F32))
        state = [init, init]
        for j in range(i + 1):
            keys = slice(j * blk, (j + 1) * blk)
            v = v_ref[keys, lanes]
            ones = jnp.ones_like(v)
            vs = (jnp.where(lane < FOX_HEAD_DIM, v, ones), jnp.where(lane < FOX_HEAD_DIM, ones, v))
            for h in range(2):
                s = _dot_nt(qs[h], k_ref[keys, lanes]) - crow_ref[pair, h:h + 1, keys] * LOG2E
                if j == i:
                    row = lax.broadcasted_iota(jnp.int32, (blk, blk), 0)
                    col = lax.broadcasted_iota(jnp.int32, (blk, blk), 1)
                    s = jnp.where(row >= col, s, -jnp.inf)
                state[h] = online(state[h], s, vs[h])
        acc0, acc1 = state[0][1], state[1][1]
        num = jnp.where(lane < FOX_HEAD_DIM, acc0, acc1)
        den = jnp.where(lane < FOX_HEAD_DIM, pltpu.roll(acc0, FOX_HEAD_DIM, axis=1), pltpu.roll(acc1, FOX_HEAD_DIM, axis=1))
        o_ref[i * blk:(i + 1) * blk, lanes] = (num / den).astype(o_ref.dtype)

    lax.fori_loop(0, crow_ref.shape[0], head_pair, 0)


def _fox_attention(z, c_row, *, batch, seq, q_col, k_col, v_col, blk):
    n = z.shape[0]
    pairs = FOX_HEADS // 2
    width = pairs * LANES

    def cols(first):
        return pl.BlockSpec((seq, width), lambda b: (b, first // width))

    return pl.pallas_call(
        functools.partial(_fox_kernel, blk=blk),
        grid=(batch,),
        in_specs=[cols(q_col), cols(k_col), cols(v_col), pl.BlockSpec((None, pairs, 2, seq), lambda b: (b, 0, 0, 0))],
        out_specs=cols(0),
        out_shape=jax.ShapeDtypeStruct((n, width), BF16),
        compiler_params=_params("parallel"),
        name="fox_attention",
    )(z, z, z, c_row)


def _shift_rows(x3, r):
    sub = lax.broadcasted_iota(jnp.int32, x3.shape[1:], 0)[None]
    rolled = pltpu.roll(x3, SUBLANES - r, axis=1)
    return jnp.where(sub < SUBLANES - r, rolled[:-1], rolled[1:])


def _mixer_kernel(x_ref, ab_ref, ac_ref, au_ref, cv_ref, cg_ref, ga_ref, gb_ref, gc_ref,
                  hac_ref, hau_ref, hcv_ref, hcg_ref, o_ref,
                  conva_ref, convc_ref, cbias_ref, lng_ref, lnb_ref,
                  woa_ref, wob_ref, woc_ref, wo_ref, out_ref,
                  ush_scr, p_scr, uc_scr, pa_scr, wc_scr, wa_scr, *, tm, seq, rows):
    d = x_ref.shape[-1]
    hg = HALO // SUBLANES
    gu = hg + tm // SUBLANES
    gc = rows // SUBLANES

    def grouped(v):
        return v.reshape(v.shape[0] // SUBLANES, SUBLANES, d)

    for k in range(CONV_C_WIDTH):
        wc_scr[k] = jnp.broadcast_to(convc_ref[k:k + 1, :], (SUBLANES, d))
    for k in range(CONV_A_WIDTH):
        wa_scr[k] = jnp.broadcast_to(conva_ref[k:k + 1, :], (SUBLANES, d))

    keep = jnp.where((pl.program_id(0) * tm) % seq == 0, 0.0, 1.0)
    ush_scr[0, 0:hg] = grouped(hcv_ref[...].astype(F32) * hcg_ref[...].astype(F32) * keep)
    ush_scr[0, hg:] = grouped(cv_ref[...].astype(F32) * cg_ref[...].astype(F32))
    p_scr[0:hg] = grouped(hac_ref[...].astype(F32) * hau_ref[...].astype(F32) * keep)
    p_scr[hg:] = grouped(ac_ref[...].astype(F32) * au_ref[...].astype(F32))
    u0 = ush_scr[0]
    for r in range(1, SUBLANES):
        ush_scr[r, 0:gu - 1] = _shift_rows(u0, r)

    for c in range(tm // rows):
        g0 = c * gc
        r0 = c * rows
        acc = jnp.broadcast_to(cbias_ref[...][None], (gc, SUBLANES, d))
        for k in range(CONV_C_WIDTH):
            a, r = divmod(HALO - (CONV_C_WIDTH - 1) + k, SUBLANES)
            acc = acc + wc_scr[k][None] * ush_scr[r, g0 + a:g0 + a + gc]
        mu = jnp.mean(acc, axis=-1, keepdims=True)
        xc = acc - mu
        y = xc * lax.rsqrt(jnp.mean(xc * xc, axis=-1, keepdims=True) + EPS) * lng_ref[...][None] + lnb_ref[...][None]
        uc_scr[r0:r0 + rows, :] = (y * _sigmoid(y)).reshape(rows, d).astype(BF16)
        acc = None
        for k in range(CONV_A_WIDTH):
            a, r = divmod(HALO - (CONV_A_WIDTH - 1) + k, SUBLANES)
            if r == 0:
                tap = p_scr[g0 + a:g0 + a + gc]
            else:
                tap = _shift_rows(p_scr[g0 + a:g0 + a + gc + 1], r)
            term = wa_scr[k][None] * tap
            acc = term if acc is None else acc + term
        pa_scr[r0:r0 + rows, :] = (ab_ref[r0:r0 + rows, :].astype(F32) * acc.reshape(rows, d)).astype(BF16)

    merged = ga_ref[...].astype(F32) * _dot(pa_scr[...], woa_ref[...])
    merged = merged + gb_ref[...].astype(F32) * _dot(o_ref[...], wob_ref[...])
    merged = merged + gc_ref[...].astype(F32) * _dot(uc_scr[...], woc_ref[...])
    out_ref[...] = x_ref[...] + _dot(merged.astype(BF16), wo_ref[...])


def _mixer(x, z, o, conv_a, conv_c, conv_c_bias, ln_g, ln_b, w_out_a, w_out_b, w_out_c, w_o,
           layer, cols, *, seq, tm, rows=32):
    n, d = x.shape
    hb = tm // HALO
    cb = {name: c // d for name, c in cols.items()}

    def zspec(name):
        j = cb[name]
        return pl.BlockSpec((tm, d), lambda i: (i, j))

    def hspec(name):
        j = cb[name]
        return pl.BlockSpec((HALO, d), lambda i: (jnp.maximum(i * hb - 1, 0), j))

    def vec(k):
        return pl.BlockSpec((None, k, d), lambda i: (layer, 0, 0))

    wspec = pl.BlockSpec((None, d, d), lambda i: (layer, 0, 0))
    tile = pl.BlockSpec((tm, d), lambda i: (i, 0))
    return pl.pallas_call(
        functools.partial(_mixer_kernel, tm=tm, seq=seq, rows=rows),
        grid=(n // tm,),
        in_specs=[tile] + [zspec(k) for k in ("a_b", "a_c", "a_u", "c_val", "c_gate", "g_a", "g_b", "g_c")]
        + [hspec(k) for k in ("a_c", "a_u", "c_val", "c_gate")] + [tile]
        + [vec(CONV_A_WIDTH), vec(CONV_C_WIDTH), vec(1), vec(1), vec(1)]
        + [wspec] * 4,
        out_specs=tile,
        out_shape=jax.ShapeDtypeStruct((n, d), F32),
        scratch_shapes=[
            pltpu.VMEM((SUBLANES, (tm + HALO) // SUBLANES, SUBLANES, d), F32),
            pltpu.VMEM(((tm + HALO) // SUBLANES, SUBLANES, d), F32),
            pltpu.VMEM((tm, d), BF16),
            pltpu.VMEM((tm, d), BF16),
            pltpu.VMEM((CONV_C_WIDTH, SUBLANES, d), F32),
            pltpu.VMEM((CONV_A_WIDTH, SUBLANES, d), F32),
        ],
        compiler_params=_params("parallel"),
        name="mixer_merge",
    )(x, *([z] * 12), o, conv_a, conv_c, conv_c_bias, ln_g, ln_b, w_out_a, w_out_b, w_out_c, w_o)


def _mem_kv_kernel(mem_ref, g_ref, w_ref, kv_ref):
    kv_ref[...] = _dot(_rms_norm(mem_ref[...], g_ref[...]).astype(BF16), w_ref[...]).astype(kv_ref.dtype)


def _mem_kv(mem, gain, w_xkv):
    m, d = mem.shape
    depth, _, e = w_xkv.shape
    return pl.pallas_call(
        _mem_kv_kernel,
        grid=(depth,),
        in_specs=[
            pl.BlockSpec((m, d), lambda l: (0, 0)),
            pl.BlockSpec((1, d), lambda l: (0, 0)),
            pl.BlockSpec((None, d, e), lambda l: (l, 0, 0)),
        ],
        out_specs=pl.BlockSpec((None, m, e), lambda l: (l, 0, 0)),
        out_shape=jax.ShapeDtypeStruct((depth, m, e), BF16),
        compiler_params=_params("parallel"),
        name="mem_kv",
    )(mem, gain, w_xkv)


def _xattn_kernel(x_ref, g_ref, wq_ref, kv_ref, wo_ref, out_ref):
    x = x_ref[...]
    q = _dot(_rms_norm(x, g_ref[...]).astype(BF16), wq_ref[...]) * (X_HEAD_DIM ** -0.5)
    q = q.astype(BF16)
    dx = X_HEADS * X_HEAD_DIM
    outs = []
    for h in range(X_HEADS):
        sl = slice(h * X_HEAD_DIM, (h + 1) * X_HEAD_DIM)
        s = _dot_nt(q[:, sl], kv_ref[:, sl])
        p = jnp.exp(s - jnp.max(s, axis=-1, keepdims=True))
        oh = _dot(p.astype(BF16), kv_ref[:, dx + h * X_HEAD_DIM:dx + (h + 1) * X_HEAD_DIM])
        outs.append((oh / jnp.sum(p, axis=-1, keepdims=True)).astype(BF16))
    out_ref[...] = x + _dot(jnp.concatenate(outs, axis=-1), wo_ref[...])


def _xattn(x, gain, w_xq, kv, w_xo, layer, *, seq, n_mem, tm):
    n, d = x.shape
    dx = w_xq.shape[-1]
    per_seq = seq // tm
    tile = pl.BlockSpec((tm, d), lambda i: (i, 0))
    return pl.pallas_call(
        _xattn_kernel,
        grid=(n // tm,),
        in_specs=[
            tile,
            pl.BlockSpec((None, 1, d), lambda i: (layer, 0, 0)),
            pl.BlockSpec((None, d, dx), lambda i: (layer, 0, 0)),
            pl.BlockSpec((None, None, n_mem, 2 * dx), lambda i: (layer, i // per_seq, 0, 0)),
            pl.BlockSpec((None, dx, d), lambda i: (layer, 0, 0)),
        ],
        out_specs=tile,
        out_shape=jax.ShapeDtypeStruct((n, d), F32),
        compiler_params=_params("parallel"),
        name="xattn",
    )(x, gain, w_xq, kv, w_xo)


def _ffn_kernel(x_ref, g_ref, wg_ref, wu_ref, wd_ref, fg_ref, out_ref, *, final, chunks):
    x = x_ref[...]
    xn = _rms_norm(x, g_ref[...]).astype(BF16)
    y = x
    for c0, c1 in chunks:
        g = _dot(xn, wg_ref[:, c0:c1])
        h = (g * _sigmoid(g) * _dot(xn, wu_ref[:, c0:c1])).astype(BF16)
        y = y + _dot(h, wd_ref[c0:c1, :])
    if final:
        y = _rms_norm(y, fg_ref[...])
    out_ref[...] = y


def _ffn_chunks(dff):
    tiles = dff // MXU_WIDTH
    assert tiles * MXU_WIDTH == dff
    half = (tiles + 1) // 2
    return tuple((a * MXU_WIDTH, b * MXU_WIDTH) for a, b in ((0, half), (half, tiles)) if b > a)


def _ffn(x, gain, w_gate_up, w_down, final_gain, layer, *, tm, final):
    n, d = x.shape
    dff = w_down.shape[1]
    tile = pl.BlockSpec((tm, d), lambda i: (i, 0))
    once = pl.Buffered(1)
    return pl.pallas_call(
        functools.partial(_ffn_kernel, final=final, chunks=_ffn_chunks(dff)),
        grid=(n // tm,),
        in_specs=[
            tile,
            pl.BlockSpec((None, 1, d), lambda i: (layer, 0, 0)),
            pl.BlockSpec((None, d, dff), lambda i: (layer, 0, 0), pipeline_mode=once),
            pl.BlockSpec((None, d, dff), lambda i: (layer, 0, 1), pipeline_mode=once),
            pl.BlockSpec((None, dff, d), lambda i: (layer, 0, 0), pipeline_mode=once),
            pl.BlockSpec((1, d), lambda i: (0, 0)),
        ],
        out_specs=tile,
        out_shape=jax.ShapeDtypeStruct((n, d), F32),
        compiler_params=_params("parallel"),
        name="ffn",
    )(x, gain, w_gate_up, w_gate_up, w_down, final_gain)


def _pick(n, pref):
    t = min(n, pref)
    assert n % t == 0, (n, t)
    return t


def kernel(x, mem, mix_norm, w_in, b_gate, b_forget, conv_a, w_out_a, w_out_b, conv_c, conv_c_bias,
           ln_c_gain, ln_c_bias, w_out_c, w_o, xattn_norm, mem_norm, w_xq, w_xkv, w_xo, ffn_norm,
           w_gate_up, w_down, final_norm):
    batch, seq, d = x.shape
    depth = w_in.shape[0]
    n_mem = mem.shape[1]
    n = batch * seq
    d_b = FOX_HEADS * FOX_HEAD_DIM
    assert d % LANES == 0 and seq % CUM_BLOCK == 0 and d_b == d

    names = ("a_b", "a_c", "a_u", "q", "k", "v", "c_val", "c_gate", "g_a", "g_b", "g_c")
    cols = {name: idx * d for idx, name in enumerate(names)}
    f_col = 6 * d
    w_in16 = w_in.astype(BF16)
    w_main = jnp.concatenate([w_in16[:, :, :f_col], w_in16[:, :, f_col + FOX_HEADS:]], axis=-1)
    e_main = w_main.shape[-1]
    w_f = jnp.pad(w_in16[:, :, f_col:f_col + FOX_HEADS], ((0, 0), (0, 0), (0, LANES - FOX_HEADS)))
    b_f = jnp.pad(b_forget, ((0, 0), (0, LANES - FOX_HEADS)))[:, None, :]
    tri = jnp.tril(jnp.ones((CUM_BLOCK, CUM_BLOCK), BF16))
    col_scale = jnp.ones((1, e_main), F32).at[:, cols["q"]:cols["q"] + d_b].set(
        FOX_HEAD_DIM ** -0.5 * LOG2E)

    row3 = lambda a: a[:, None, :]
    bf = lambda a: a.astype(BF16)
    gate_col = cols["c_gate"]
    col_bias = jnp.zeros((depth, 1, e_main), F32).at[:, 0, cols["g_a"]:].set(b_gate)

    kv = _mem_kv(mem.reshape(batch * n_mem, d), mem_norm[None, :], bf(w_xkv))
    kv = kv.reshape(depth, batch, n_mem, kv.shape[-1])
    w_oa, w_ob, w_oc, w_oo = bf(w_out_a), bf(w_out_b), bf(w_out_c), bf(w_o)
    w_q, w_xo_b, w_gu, w_dn = bf(w_xq), bf(w_xo), bf(w_gate_up), bf(w_down)

    xs = x.reshape(n, d)
    for l in range(depth):
        z, f = _in_proj(xs, row3(mix_norm), w_main, w_f, col_scale, col_bias, l, tm=_pick(n, 2048),
                        tn=_pick(d, 1024), gate_col=gate_col)
        c_row = _forget_cumsum(f, b_f, l, tri, batch=batch, seq=seq)
        c_row = c_row.reshape(batch, FOX_HEADS // 2, 2, seq)
        o = _fox_attention(z, c_row, batch=batch, seq=seq, q_col=cols["q"], k_col=cols["k"],
                           v_col=cols["v"], blk=_pick(seq, 512))
        xs = _mixer(xs, z, o, conv_a, conv_c, row3(conv_c_bias), row3(ln_c_gain), row3(ln_c_bias),
                    w_oa, w_ob, w_oc, w_oo, l, cols, seq=seq, tm=_pick(seq, 256))
        xs = _xattn(xs, row3(xattn_norm), w_q, kv, w_xo_b, l, seq=seq, n_mem=n_mem, tm=_pick(seq, 512))
        xs = _ffn(xs, row3(ffn_norm), w_gu, w_dn, final_norm[None, :], l, tm=_pick(n, 512),
                  final=(l == depth - 1))
    return xs.reshape(batch, seq, d)
```

```python
import functools

import jax
import jax.numpy as jnp
from jax import lax
from jax.experimental import pallas as pl
from jax.experimental.pallas import tpu as pltpu

EPS = 1e-6
FOX_HEADS = 16
FOX_HEAD_DIM = 64
X_HEADS = 4
X_HEAD_DIM = 128
CONV_A_WIDTH = 3
CONV_C_WIDTH = 31
LANES = 128
HALO = 32
MXU_WIDTH = 256
CUM_BLOCK = MXU_WIDTH
VMEM_LIMIT = 56 * 1024 * 1024
SUBLANES = 8
LOG2E = 1.4426950408889634

F32 = jnp.float32
BF16 = jnp.bfloat16


def _params(*sem):
    return pltpu.CompilerParams(dimension_semantics=sem, vmem_limit_bytes=VMEM_LIMIT)


def _rms_norm(x, g):
    return x * lax.rsqrt(jnp.mean(x * x, axis=-1, keepdims=True) + EPS) * g


def _sigmoid(x):
    return 0.5 * jnp.tanh(0.5 * x) + 0.5


def _dot(a, b):
    return jnp.dot(a, b, preferred_element_type=F32)


def _dot_nt(a, b):
    return lax.dot_general(a, b, (((1,), (1,)), ((), ())), preferred_element_type=F32)


def _in_proj_kernel(x_ref, g_ref, wa_ref, wb_ref, wf_ref, cs_ref, cb_ref, z_ref, f_ref, xn_ref,
                    *, skip, skip_tile, gate_tile):
    j = pl.program_id(1)

    def weights(shifted):
        w = wa_ref[...]
        if shifted:
            w = jnp.concatenate([w[:, skip:], wb_ref[:, :skip]], axis=1)
        return w.astype(BF16)

    @pl.when(j == 0)
    def _():
        xn = _rms_norm(x_ref[...], g_ref[...]).astype(BF16)
        xn_ref[...] = xn
        f_ref[...] = _dot(xn, wf_ref[...])

    @pl.when(j < skip_tile)
    def _():
        z_ref[...] = (_dot(xn_ref[...], weights(False)) * cs_ref[...]).astype(z_ref.dtype)

    @pl.when((j >= skip_tile) & (j < gate_tile))
    def _():
        z_ref[...] = (_dot(xn_ref[...], weights(True)) * cs_ref[...]).astype(z_ref.dtype)

    @pl.when(j >= gate_tile)
    def _():
        z_ref[...] = _sigmoid(_dot(xn_ref[...], weights(True)) + cb_ref[...]).astype(z_ref.dtype)


def _in_proj(x, gain, w_in, w_f, col_scale, col_bias, layer, *, tm, tn, skip_col, skip, gate_col):
    n, d = x.shape
    e = w_in.shape[-1] - skip
    skip_tile = skip_col // tn
    assert skip_tile * tn == skip_col and skip_tile <= gate_col // tn and skip < LANES
    per_tile = tn // LANES
    return pl.pallas_call(
        functools.partial(_in_proj_kernel, skip=skip, skip_tile=skip_tile, gate_tile=gate_col // tn),
        grid=(n // tm, e // tn),
        in_specs=[
            pl.BlockSpec((tm, d), lambda i, j: (i, 0)),
            pl.BlockSpec((None, 1, d), lambda i, j: (layer, 0, 0)),
            pl.BlockSpec((None, d, tn), lambda i, j: (layer, 0, j)),
            pl.BlockSpec((None, d, LANES), lambda i, j: (layer, 0, (jnp.maximum(j, skip_tile) + 1) * per_tile)),
            pl.BlockSpec((None, d, LANES), lambda i, j: (layer, 0, 0)),
            pl.BlockSpec((1, tn), lambda i, j: (0, j)),
            pl.BlockSpec((None, 1, tn), lambda i, j: (layer, 0, j)),
        ],
        out_specs=[
            pl.BlockSpec((tm, tn), lambda i, j: (i, j)),
            pl.BlockSpec((tm, LANES), lambda i, j: (i, 0)),
        ],
        out_shape=[
            jax.ShapeDtypeStruct((n, e), BF16),
            jax.ShapeDtypeStruct((n, LANES), F32),
        ],
        scratch_shapes=[pltpu.VMEM((tm, d), BF16)],
        compiler_params=_params("parallel", "arbitrary"),
        name="in_proj",
    )(x, gain, w_in, w_in, w_f, col_scale, col_bias)


def _forget_cumsum_kernel(f_ref, b_ref, tri_ref, crow_ref, c_scr):
    x = f_ref[...] + b_ref[...]
    lf = jnp.minimum(x, 0.0) - jnp.log(1.0 + jnp.exp(-jnp.abs(x)))
    hi = lf.astype(BF16)
    rem = lf - hi.astype(F32)
    mid = rem.astype(BF16)
    lo = (rem - mid.astype(F32)).astype(BF16)
    tri = tri_ref[...]
    seq = f_ref.shape[0]
    carry = jnp.zeros((1, LANES), F32)
    for blk in range(seq // CUM_BLOCK):
        sl = slice(blk * CUM_BLOCK, (blk + 1) * CUM_BLOCK)
        part = _dot(tri, hi[sl]) + _dot(tri, mid[sl]) + _dot(tri, lo[sl]) + carry
        c_scr[sl, :] = part
        carry = part[CUM_BLOCK - 1:CUM_BLOCK, :]
    crow_ref[...] = c_scr[...].T[:FOX_HEADS, :]


def _forget_cumsum(f, b_forget, layer, tri, *, batch, seq):
    return pl.pallas_call(
        _forget_cumsum_kernel,
        grid=(batch,),
        in_specs=[
            pl.BlockSpec((seq, LANES), lambda b: (b, 0)),
            pl.BlockSpec((None, 1, LANES), lambda b: (layer, 0, 0)),
            pl.BlockSpec((CUM_BLOCK, CUM_BLOCK), lambda b: (0, 0)),
        ],
        out_specs=pl.BlockSpec((None, FOX_HEADS, seq), lambda b: (b, 0, 0)),
        out_shape=jax.ShapeDtypeStruct((batch, FOX_HEADS, seq), F32),
        scratch_shapes=[pltpu.VMEM((seq, LANES), F32)],
        compiler_params=_params("parallel"),
        name="forget_cumsum",
    )(f, b_forget, tri)


def _fox_kernel(q_ref, k_ref, v_ref, crow_ref, o_ref, *, blk):
    lane = lax.broadcasted_iota(jnp.int32, (1, LANES), 1)
    seq = k_ref.shape[0]

    def online(state, s, vblk):
        m, acc = state
        m_new = jnp.maximum(m, jnp.max(s, axis=-1, keepdims=True))
        p = jnp.exp2(s - m_new)
        return m_new, jnp.exp2(m - m_new) * acc + _dot(p.astype(BF16), vblk)

    def head_pair(pair, carry):
        lanes = pl.ds(pl.multiple_of(pair * LANES, LANES), LANES)
        for i in range(seq // blk):
            pair_block(pair, lanes, i)
        return carry

    def pair_block(pair, lanes, i):
        q = q_ref[i * blk:(i + 1) * blk, lanes]
        zero = jnp.zeros_like(q)
        qs = (jnp.where(lane < FOX_HEAD_DIM, q, zero), jnp.where(lane < FOX_HEAD_DIM, zero, q))
        init = (jnp.full((blk, 1), -jnp.inf, F32), jnp.zeros((blk, LANES), F32))
        state = [init, init]
        for j in range(i + 1):
            keys = slice(j * blk, (j + 1) * blk)
            v = v_ref[keys, lanes]
            ones = jnp.ones_like(v)
            vs = (jnp.where(lane < FOX_HEAD_DIM, v, ones), jnp.where(lane < FOX_HEAD_DIM, ones, v))
            for h in range(2):
                s = _dot_nt(qs[h], k_ref[keys, lanes]) - crow_ref[pair, h:h + 1, keys] * LOG2E
                if j == i:
                    row = lax.broadcasted_iota(jnp.int32, (blk, blk), 0)
                    col = lax.broadcasted_iota(jnp.int32, (blk, blk), 1)
                    s = jnp.where(row >= col, s, -jnp.inf)
                state[h] = online(state[h], s, vs[h])
        acc0, acc1 = state[0][1], state[1][1]
        num = jnp.where(lane < FOX_HEAD_DIM, acc0, acc1)
        den = jnp.where(lane < FOX_HEAD_DIM, pltpu.roll(acc0, FOX_HEAD_DIM, axis=1), pltpu.roll(acc1, FOX_HEAD_DIM, axis=1))
        o_ref[i * blk:(i + 1) * blk, lanes] = (num / den).astype(o_ref.dtype)

    lax.fori_loop(0, crow_ref.shape[0], head_pair, 0)


def _fox_attention(z, c_row, *, batch, seq, q_col, k_col, v_col, blk):
    n = z.shape[0]
    pairs = FOX_HEADS // 2
    width = pairs * LANES

    def cols(first):
        return pl.BlockSpec((seq, width), lambda b: (b, first // width))

    return pl.pallas_call(
        functools.partial(_fox_kernel, blk=blk),
        grid=(batch,),
        in_specs=[cols(q_col), cols(k_col), cols(v_col), pl.BlockSpec((None, pairs, 2, seq), lambda b: (b, 0, 0, 0))],
        out_specs=cols(0),
        out_shape=jax.ShapeDtypeStruct((n, width), BF16),
        compiler_params=_params("parallel"),
        name="fox_attention",
    )(z, z, z, c_row)


def _shift_rows(x3, r):
    sub = lax.broadcasted_iota(jnp.int32, x3.shape[1:], 0)[None]
    rolled = pltpu.roll(x3, SUBLANES - r, axis=1)
    return jnp.where(sub < SUBLANES - r, rolled[:-1], rolled[1:])


def _mixer_kernel(x_ref, ab_ref, ac_ref, au_ref, cv_ref, cg_ref, ga_ref, gb_ref, gc_ref,
                  hac_ref, hau_ref, hcv_ref, hcg_ref, o_ref,
                  conva_ref, convc_ref, cbias_ref, lng_ref, lnb_ref,
                  woa_ref, wob_ref, woc_ref, wo_ref, out_ref,
                  ush_scr, p_scr, uc_scr, pa_scr, wc_scr, wa_scr, *, tm, seq, rows):
    d = x_ref.shape[-1]
    hg = HALO // SUBLANES
    gu = hg + tm // SUBLANES
    gc = rows // SUBLANES

    def grouped(v):
        return v.reshape(v.shape[0] // SUBLANES, SUBLANES, d)

    for k in range(CONV_C_WIDTH):
        wc_scr[k] = jnp.broadcast_to(convc_ref[k:k + 1, :], (SUBLANES, d))
    for k in range(CONV_A_WIDTH):
        wa_scr[k] = jnp.broadcast_to(conva_ref[k:k + 1, :], (SUBLANES, d))

    keep = jnp.where((pl.program_id(0) * tm) % seq == 0, 0.0, 1.0)
    ush_scr[0, 0:hg] = grouped(hcv_ref[...].astype(F32) * hcg_ref[...].astype(F32) * keep)
    ush_scr[0, hg:] = grouped(cv_ref[...].astype(F32) * cg_ref[...].astype(F32))
    p_scr[0:hg] = grouped(hac_ref[...].astype(F32) * hau_ref[...].astype(F32) * keep)
    p_scr[hg:] = grouped(ac_ref[...].astype(F32) * au_ref[...].astype(F32))
    u0 = ush_scr[0]
    for r in range(1, SUBLANES):
        ush_scr[r, 0:gu - 1] = _shift_rows(u0, r)

    for c in range(tm // rows):
        g0 = c * gc
        r0 = c * rows
        acc = jnp.broadcast_to(cbias_ref[...][None], (gc, SUBLANES, d))
        for k in range(CONV_C_WIDTH):
            a, r = divmod(HALO - (CONV_C_WIDTH - 1) + k, SUBLANES)
            acc = acc + wc_scr[k][None] * ush_scr[r, g0 + a:g0 + a + gc]
        mu = jnp.mean(acc, axis=-1, keepdims=True)
        xc = acc - mu
        y = xc * lax.rsqrt(jnp.mean(xc * xc, axis=-1, keepdims=True) + EPS) * lng_ref[...][None] + lnb_ref[...][None]
        uc_scr[r0:r0 + rows, :] = (y * _sigmoid(y)).reshape(rows, d).astype(BF16)
        acc = None
        for k in range(CONV_A_WIDTH):
            a, r = divmod(HALO - (CONV_A_WIDTH - 1) + k, SUBLANES)
            if r == 0:
                tap = p_scr[g0 + a:g0 + a + gc]
            else:
                tap = _shift_rows(p_scr[g0 + a:g0 + a + gc + 1], r)
            term = wa_scr[k][None] * tap
            acc = term if acc is None else acc + term
        pa_scr[r0:r0 + rows, :] = (ab_ref[r0:r0 + rows, :].astype(F32) * acc.reshape(rows, d)).astype(BF16)

    merged = ga_ref[...].astype(F32) * _dot(pa_scr[...], woa_ref[...])
    merged = merged + gb_ref[...].astype(F32) * _dot(o_ref[...], wob_ref[...])
    merged = merged + gc_ref[...].astype(F32) * _dot(uc_scr[...], woc_ref[...])
    out_ref[...] = x_ref[...] + _dot(merged.astype(BF16), wo_ref[...])


def _mixer(x, z, o, conv_a, conv_c, conv_c_bias, ln_g, ln_b, w_out_a, w_out_b, w_out_c, w_o,
           layer, cols, *, seq, tm, rows=32):
    n, d = x.shape
    hb = tm // HALO
    cb = {name: c // d for name, c in cols.items()}

    def zspec(name):
        j = cb[name]
        return pl.BlockSpec((tm, d), lambda i: (i, j))

    def hspec(name):
        j = cb[name]
        return pl.BlockSpec((HALO, d), lambda i: (jnp.maximum(i * hb - 1, 0), j))

    def vec(k):
        return pl.BlockSpec((None, k, d), lambda i: (layer, 0, 0))

    wspec = pl.BlockSpec((None, d, d), lambda i: (layer, 0, 0))
    tile = pl.BlockSpec((tm, d), lambda i: (i, 0))
    return pl.pallas_call(
        functools.partial(_mixer_kernel, tm=tm, seq=seq, rows=rows),
        grid=(n // tm,),
        in_specs=[tile] + [zspec(k) for k in ("a_b", "a_c", "a_u", "c_val", "c_gate", "g_a", "g_b", "g_c")]
        + [hspec(k) for k in ("a_c", "a_u", "c_val", "c_gate")] + [tile]
        + [vec(CONV_A_WIDTH), vec(CONV_C_WIDTH), vec(1), vec(1), vec(1)]
        + [wspec] * 4,
        out_specs=tile,
        out_shape=jax.ShapeDtypeStruct((n, d), F32),
        scratch_shapes=[
            pltpu.VMEM((SUBLANES, (tm + HALO) // SUBLANES, SUBLANES, d), F32),
            pltpu.VMEM(((tm + HALO) // SUBLANES, SUBLANES, d), F32),
            pltpu.VMEM((tm, d), BF16),
            pltpu.VMEM((tm, d), BF16),
            pltpu.VMEM((CONV_C_WIDTH, SUBLANES, d), F32),
            pltpu.VMEM((CONV_A_WIDTH, SUBLANES, d), F32),
        ],
        compiler_params=_params("parallel"),
        name="mixer_merge",
    )(x, *([z] * 12), o, conv_a, conv_c, conv_c_bias, ln_g, ln_b, w_out_a, w_out_b, w_out_c, w_o)


def _mem_kv_kernel(mem_ref, g_ref, w_ref, kv_ref):
    kv_ref[...] = _dot(_rms_norm(mem_ref[...], g_ref[...]).astype(BF16), w_ref[...]).astype(kv_ref.dtype)


def _mem_kv(mem, gain, w_xkv):
    m, d = mem.shape
    depth, _, e = w_xkv.shape
    return pl.pallas_call(
        _mem_kv_kernel,
        grid=(depth,),
        in_specs=[
            pl.BlockSpec((m, d), lambda l: (0, 0)),
            pl.BlockSpec((1, d), lambda l: (0, 0)),
            pl.BlockSpec((None, d, e), lambda l: (l, 0, 0)),
        ],
        out_specs=pl.BlockSpec((None, m, e), lambda l: (l, 0, 0)),
        out_shape=jax.ShapeDtypeStruct((depth, m, e), BF16),
        compiler_params=_params("parallel"),
        name="mem_kv",
    )(mem, gain, w_xkv)


def _xattn_kernel(x_ref, g_ref, wq_ref, kv_ref, wo_ref, out_ref):
    x = x_ref[...]
    q = _dot(_rms_norm(x, g_ref[...]).astype(BF16), wq_ref[...]) * (X_HEAD_DIM ** -0.5)
    q = q.astype(BF16)
    dx = X_HEADS * X_HEAD_DIM
    outs = []
    for h in range(X_HEADS):
        sl = slice(h * X_HEAD_DIM, (h + 1) * X_HEAD_DIM)
        s = _dot_nt(q[:, sl], kv_ref[:, sl])
        p = jnp.exp(s - jnp.max(s, axis=-1, keepdims=True))
        oh = _dot(p.astype(BF16), kv_ref[:, dx + h * X_HEAD_DIM:dx + (h + 1) * X_HEAD_DIM])
        outs.append((oh / jnp.sum(p, axis=-1, keepdims=True)).astype(BF16))
    out_ref[...] = x + _dot(jnp.concatenate(outs, axis=-1), wo_ref[...])


def _xattn(x, gain, w_xq, kv, w_xo, layer, *, seq, n_mem, tm):
    n, d = x.shape
    dx = w_xq.shape[-1]
    per_seq = seq // tm
    tile = pl.BlockSpec((tm, d), lambda i: (i, 0))
    return pl.pallas_call(
        _xattn_kernel,
        grid=(n // tm,),
        in_specs=[
            tile,
            pl.BlockSpec((None, 1, d), lambda i: (layer, 0, 0)),
            pl.BlockSpec((None, d, dx), lambda i: (layer, 0, 0)),
            pl.BlockSpec((None, None, n_mem, 2 * dx), lambda i: (layer, i // per_seq, 0, 0)),
            pl.BlockSpec((None, dx, d), lambda i: (layer, 0, 0)),
        ],
        out_specs=tile,
        out_shape=jax.ShapeDtypeStruct((n, d), F32),
        compiler_params=_params("parallel"),
        name="xattn",
    )(x, gain, w_xq, kv, w_xo)


def _ffn_kernel(x_ref, g_ref, wg_ref, wu_ref, wd_ref, fg_ref, out_ref, *, final, chunks):
    x = x_ref[...]
    xn = _rms_norm(x, g_ref[...]).astype(BF16)
    y = x
    for c0, c1 in chunks:
        g = _dot(xn, wg_ref[:, c0:c1])
        h = (g * _sigmoid(g) * _dot(xn, wu_ref[:, c0:c1])).astype(BF16)
        y = y + _dot(h, wd_ref[c0:c1, :])
    if final:
        y = _rms_norm(y, fg_ref[...])
    out_ref[...] = y


def _ffn_chunks(dff):
    tiles = dff // MXU_WIDTH
    assert tiles * MXU_WIDTH == dff
    half = (tiles + 1) // 2
    return tuple((a * MXU_WIDTH, b * MXU_WIDTH) for a, b in ((0, half), (half, tiles)) if b > a)


def _ffn(x, gain, w_gate_up, w_down, final_gain, layer, *, tm, final):
    n, d = x.shape
    dff = w_down.shape[1]
    tile = pl.BlockSpec((tm, d), lambda i: (i, 0))
    once = pl.Buffered(1)
    return pl.pallas_call(
        functools.partial(_ffn_kernel, final=final, chunks=_ffn_chunks(dff)),
        grid=(n // tm,),
        in_specs=[
            tile,
            pl.BlockSpec((None, 1, d), lambda i: (layer, 0, 0)),
            pl.BlockSpec((None, d, dff), lambda i: (layer, 0, 0), pipeline_mode=once),
            pl.BlockSpec((None, d, dff), lambda i: (layer, 0, 1), pipeline_mode=once),
            pl.BlockSpec((None, dff, d), lambda i: (layer, 0, 0), pipeline_mode=once),
            pl.BlockSpec((1, d), lambda i: (0, 0)),
        ],
        out_specs=tile,
        out_shape=jax.ShapeDtypeStruct((n, d), F32),
        compiler_params=_params("parallel"),
        name="ffn",
    )(x, gain, w_gate_up, w_gate_up, w_down, final_gain)


def _pick(n, pref):
    t = min(n, pref)
    assert n % t == 0, (n, t)
    return t


def kernel(x, mem, mix_norm, w_in, b_gate, b_forget, conv_a, w_out_a, w_out_b, conv_c, conv_c_bias,
           ln_c_gain, ln_c_bias, w_out_c, w_o, xattn_norm, mem_norm, w_xq, w_xkv, w_xo, ffn_norm,
           w_gate_up, w_down, final_norm):
    batch, seq, d = x.shape
    depth = w_in.shape[0]
    n_mem = mem.shape[1]
    n = batch * seq
    d_b = FOX_HEADS * FOX_HEAD_DIM
    assert d % LANES == 0 and seq % CUM_BLOCK == 0 and d_b == d

    names = ("a_b", "a_c", "a_u", "q", "k", "v", "c_val", "c_gate", "g_a", "g_b", "g_c")
    cols = {name: idx * d for idx, name in enumerate(names)}
    f_col = 6 * d
    e_main = w_in.shape[-1] - FOX_HEADS
    w_f = jnp.pad(w_in[:, :, f_col:f_col + FOX_HEADS], ((0, 0), (0, 0), (0, LANES - FOX_HEADS))).astype(BF16)
    b_f = jnp.pad(b_forget, ((0, 0), (0, LANES - FOX_HEADS)))[:, None, :]
    tri = jnp.tril(jnp.ones((CUM_BLOCK, CUM_BLOCK), BF16))
    col_scale = jnp.ones((1, e_main), F32).at[:, cols["q"]:cols["q"] + d_b].set(
        FOX_HEAD_DIM ** -0.5 * LOG2E)

    row3 = lambda a: a[:, None, :]
    bf = lambda a: a.astype(BF16)
    gate_col = cols["c_gate"]
    col_bias = jnp.zeros((depth, 1, e_main), F32).at[:, 0, cols["g_a"]:].set(b_gate)

    kv = _mem_kv(mem.reshape(batch * n_mem, d), mem_norm[None, :], bf(w_xkv))
    kv = kv.reshape(depth, batch, n_mem, kv.shape[-1])
    w_oa, w_ob, w_oc, w_oo = bf(w_out_a), bf(w_out_b), bf(w_out_c), bf(w_o)
    w_q, w_xo_b, w_gu, w_dn = bf(w_xq), bf(w_xo), bf(w_gate_up), bf(w_down)

    xs = x.reshape(n, d)
    for l in range(depth):
        z, f = _in_proj(xs, row3(mix_norm), w_in, w_f, col_scale, col_bias, l, tm=_pick(n, 2048),
                        tn=_pick(d, 1024), skip_col=f_col, skip=FOX_HEADS, gate_col=gate_col)
        c_row = _forget_cumsum(f, b_f, l, tri, batch=batch, seq=seq)
        c_row = c_row.reshape(batch, FOX_HEADS // 2, 2, seq)
        o = _fox_attention(z, c_row, batch=batch, seq=seq, q_col=cols["q"], k_col=cols["k"],
                           v_col=cols["v"], blk=_pick(seq, 512))
        xs = _mixer(xs, z, o, conv_a, conv_c, row3(conv_c_bias), row3(ln_c_gain), row3(ln_c_bias),
                    w_oa, w_ob, w_oc, w_oo, l, cols, seq=seq, tm=_pick(seq, 256))
        xs = _xattn(xs, row3(xattn_norm), w_q, kv, w_xo_b, l, seq=seq, n_mem=n_mem, tm=_pick(seq, 1024))
        xs = _ffn(xs, row3(ffn_norm), w_gu, w_dn, final_norm[None, :], l, tm=_pick(n, 512),
                  final=(l == depth - 1))
    return xs.reshape(batch, seq, d)
```

```python
import functools

import jax
import jax.numpy as jnp
from jax import lax
from jax.experimental import pallas as pl
from jax.experimental.pallas import tpu as pltpu

EPS = 1e-6
FOX_HEADS = 16
FOX_HEAD_DIM = 64
X_HEADS = 4
X_HEAD_DIM = 128
CONV_A_WIDTH = 3
CONV_C_WIDTH = 31
LANES = 128
HALO = 32
MXU_WIDTH = 256
CUM_BLOCK = MXU_WIDTH
VMEM_LIMIT = 56 * 1024 * 1024
SUBLANES = 8
LOG2E = 1.4426950408889634

F32 = jnp.float32
BF16 = jnp.bfloat16


def _params(*sem, fuse=None):
    return pltpu.CompilerParams(dimension_semantics=sem, vmem_limit_bytes=VMEM_LIMIT, allow_input_fusion=fuse)


def _rms_norm(x, g):
    return x * lax.rsqrt(jnp.mean(x * x, axis=-1, keepdims=True) + EPS) * g


def _sigmoid(x):
    return 0.5 * jnp.tanh(0.5 * x) + 0.5


def _dot(a, b):
    return jnp.dot(a, b, preferred_element_type=F32)


def _dot_nt(a, b):
    return lax.dot_general(a, b, (((1,), (1,)), ((), ())), preferred_element_type=F32)


def _in_proj_kernel(x_ref, g_ref, w_ref, wf_ref, cs_ref, cb_ref, z_ref, f_ref, xn_ref, *, gate_tile):
    j = pl.program_id(1)

    @pl.when(j == 0)
    def _():
        xn = _rms_norm(x_ref[...], g_ref[...]).astype(BF16)
        xn_ref[...] = xn
        f_ref[...] = _dot(xn, wf_ref[...])

    @pl.when(j < gate_tile)
    def _():
        z_ref[...] = (_dot(xn_ref[...], w_ref[...]) * cs_ref[...]).astype(z_ref.dtype)

    @pl.when(j >= gate_tile)
    def _():
        z_ref[...] = _sigmoid(_dot(xn_ref[...], w_ref[...]) + cb_ref[...]).astype(z_ref.dtype)


def _in_proj(x, gain, w_main, w_f, col_scale, col_bias, layer, *, tm, tn, gate_col):
    n, d = x.shape
    e = w_main.shape[-1]
    return pl.pallas_call(
        functools.partial(_in_proj_kernel, gate_tile=gate_col // tn),
        grid=(n // tm, e // tn),
        in_specs=[
            pl.BlockSpec((tm, d), lambda i, j: (i, 0)),
            pl.BlockSpec((None, 1, d), lambda i, j: (layer, 0, 0)),
            pl.BlockSpec((None, d, tn), lambda i, j: (layer, 0, j)),
            pl.BlockSpec((None, d, LANES), lambda i, j: (layer, 0, 0)),
            pl.BlockSpec((1, tn), lambda i, j: (0, j)),
            pl.BlockSpec((None, 1, tn), lambda i, j: (layer, 0, j)),
        ],
        out_specs=[
            pl.BlockSpec((tm, tn), lambda i, j: (i, j)),
            pl.BlockSpec((tm, LANES), lambda i, j: (i, 0)),
        ],
        out_shape=[
            jax.ShapeDtypeStruct((n, e), BF16),
            jax.ShapeDtypeStruct((n, LANES), F32),
        ],
        scratch_shapes=[pltpu.VMEM((tm, d), BF16)],
        compiler_params=_params("parallel", "arbitrary", fuse=[False, False, True, True, False, False]),
        name="in_proj",
    )(x, gain, w_main, w_f, col_scale, col_bias)


def _forget_cumsum_kernel(f_ref, b_ref, tri_ref, crow_ref, c_scr):
    x = f_ref[...] + b_ref[...]
    lf = jnp.minimum(x, 0.0) - jnp.log(1.0 + jnp.exp(-jnp.abs(x)))
    hi = lf.astype(BF16)
    rem = lf - hi.astype(F32)
    mid = rem.astype(BF16)
    lo = (rem - mid.astype(F32)).astype(BF16)
    tri = tri_ref[...]
    seq = f_ref.shape[0]
    carry = jnp.zeros((1, LANES), F32)
    for blk in range(seq // CUM_BLOCK):
        sl = slice(blk * CUM_BLOCK, (blk + 1) * CUM_BLOCK)
        part = _dot(tri, hi[sl]) + _dot(tri, mid[sl]) + _dot(tri, lo[sl]) + carry
        c_scr[sl, :] = part
        carry = part[CUM_BLOCK - 1:CUM_BLOCK, :]
    crow_ref[...] = c_scr[...].T[:FOX_HEADS, :]


def _forget_cumsum(f, b_forget, layer, tri, *, batch, seq):
    return pl.pallas_call(
        _forget_cumsum_kernel,
        grid=(batch,),
        in_specs=[
            pl.BlockSpec((seq, LANES), lambda b: (b, 0)),
            pl.BlockSpec((None, 1, LANES), lambda b: (layer, 0, 0)),
            pl.BlockSpec((CUM_BLOCK, CUM_BLOCK), lambda b: (0, 0)),
        ],
        out_specs=pl.BlockSpec((None, FOX_HEADS, seq), lambda b: (b, 0, 0)),
        out_shape=jax.ShapeDtypeStruct((batch, FOX_HEADS, seq), F32),
        scratch_shapes=[pltpu.VMEM((seq, LANES), F32)],
        compiler_params=_params("parallel"),
        name="forget_cumsum",
    )(f, b_forget, tri)


def _fox_kernel(q_ref, k_ref, v_ref, crow_ref, o_ref, *, blk):
    lane = lax.broadcasted_iota(jnp.int32, (1, LANES), 1)
    seq = k_ref.shape[0]

    def online(state, s, vblk):
        m, acc = state
        m_new = jnp.maximum(m, jnp.max(s, axis=-1, keepdims=True))
        p = jnp.exp2(s - m_new)
        return m_new, jnp.exp2(m - m_new) * acc + _dot(p.astype(BF16), vblk)

    def head_pair(pair, carry):
        lanes = pl.ds(pl.multiple_of(pair * LANES, LANES), LANES)
        for i in range(seq // blk):
            pair_block(pair, lanes, i)
        return carry

    def pair_block(pair, lanes, i):
        q = q_ref[i * blk:(i + 1) * blk, lanes]
        zero = jnp.zeros_like(q)
        qs = (jnp.where(lane < FOX_HEAD_DIM, q, zero), jnp.where(lane < FOX_HEAD_DIM, zero, q))
        init = (jnp.full((blk, 1), -jnp.inf, F32), jnp.zeros((blk, LANES), F32))
        state = [init, init]
        for j in range(i + 1):
            keys = slice(j * blk, (j + 1) * blk)
            v = v_ref[keys, lanes]
            ones = jnp.ones_like(v)
            vs = (jnp.where(lane < FOX_HEAD_DIM, v, ones), jnp.where(lane < FOX_HEAD_DIM, ones, v))
            for h in range(2):
                s = _dot_nt(qs[h], k_ref[keys, lanes]) - crow_ref[pair, h:h + 1, keys] * LOG2E
                if j == i:
                    row = lax.broadcasted_iota(jnp.int32, (blk, blk), 0)
                    col = lax.broadcasted_iota(jnp.int32, (blk, blk), 1)
                    s = jnp.where(row >= col, s, -jnp.inf)
                state[h] = online(state[h], s, vs[h])
        acc0, acc1 = state[0][1], state[1][1]
        num = jnp.where(lane < FOX_HEAD_DIM, acc0, acc1)
        den = jnp.where(lane < FOX_HEAD_DIM, pltpu.roll(acc0, FOX_HEAD_DIM, axis=1), pltpu.roll(acc1, FOX_HEAD_DIM, axis=1))
        o_ref[i * blk:(i + 1) * blk, lanes] = (num / den).astype(o_ref.dtype)

    lax.fori_loop(0, crow_ref.shape[0], head_pair, 0)


def _fox_attention(z, c_row, *, batch, seq, q_col, k_col, v_col, blk):
    n = z.shape[0]
    pairs = FOX_HEADS // 2
    width = pairs * LANES

    def cols(first):
        return pl.BlockSpec((seq, width), lambda b: (b, first // width))

    return pl.pallas_call(
        functools.partial(_fox_kernel, blk=blk),
        grid=(batch,),
        in_specs=[cols(q_col), cols(k_col), cols(v_col), pl.BlockSpec((None, pairs, 2, seq), lambda b: (b, 0, 0, 0))],
        out_specs=cols(0),
        out_shape=jax.ShapeDtypeStruct((n, width), BF16),
        compiler_params=_params("parallel"),
        name="fox_attention",
    )(z, z, z, c_row)


def _shift_rows(x3, r):
    sub = lax.broadcasted_iota(jnp.int32, x3.shape[1:], 0)[None]
    rolled = pltpu.roll(x3, SUBLANES - r, axis=1)
    return jnp.where(sub < SUBLANES - r, rolled[:-1], rolled[1:])


def _mixer_kernel(x_ref, ab_ref, ac_ref, au_ref, cv_ref, cg_ref, ga_ref, gb_ref, gc_ref,
                  hac_ref, hau_ref, hcv_ref, hcg_ref, o_ref,
                  conva_ref, convc_ref, cbias_ref, lng_ref, lnb_ref,
                  woa_ref, wob_ref, woc_ref, wo_ref, out_ref,
                  ush_scr, p_scr, uc_scr, pa_scr, wc_scr, wa_scr, *, tm, seq, rows):
    d = x_ref.shape[-1]
    hg = HALO // SUBLANES
    gu = hg + tm // SUBLANES
    gc = rows // SUBLANES

    def grouped(v):
        return v.reshape(v.shape[0] // SUBLANES, SUBLANES, d)

    for k in range(CONV_C_WIDTH):
        wc_scr[k] = jnp.broadcast_to(convc_ref[k:k + 1, :], (SUBLANES, d))
    for k in range(CONV_A_WIDTH):
        wa_scr[k] = jnp.broadcast_to(conva_ref[k:k + 1, :], (SUBLANES, d))

    keep = jnp.where((pl.program_id(0) * tm) % seq == 0, 0.0, 1.0)
    ush_scr[0, 0:hg] = grouped(hcv_ref[...].astype(F32) * hcg_ref[...].astype(F32) * keep)
    ush_scr[0, hg:] = grouped(cv_ref[...].astype(F32) * cg_ref[...].astype(F32))
    p_scr[0:hg] = grouped(hac_ref[...].astype(F32) * hau_ref[...].astype(F32) * keep)
    p_scr[hg:] = grouped(ac_ref[...].astype(F32) * au_ref[...].astype(F32))
    u0 = ush_scr[0]
    for r in range(1, SUBLANES):
        ush_scr[r, 0:gu - 1] = _shift_rows(u0, r)

    for c in range(tm // rows):
        g0 = c * gc
        r0 = c * rows
        acc = jnp.broadcast_to(cbias_ref[...][None], (gc, SUBLANES, d))
        for k in range(CONV_C_WIDTH):
            a, r = divmod(HALO - (CONV_C_WIDTH - 1) + k, SUBLANES)
            acc = acc + wc_scr[k][None] * ush_scr[r, g0 + a:g0 + a + gc]
        mu = jnp.mean(acc, axis=-1, keepdims=True)
        xc = acc - mu
        y = xc * lax.rsqrt(jnp.mean(xc * xc, axis=-1, keepdims=True) + EPS) * lng_ref[...][None] + lnb_ref[...][None]
        uc_scr[r0:r0 + rows, :] = (y * _sigmoid(y)).reshape(rows, d).astype(BF16)
        acc = None
        for k in range(CONV_A_WIDTH):
            a, r = divmod(HALO - (CONV_A_WIDTH - 1) + k, SUBLANES)
            if r == 0:
                tap = p_scr[g0 + a:g0 + a + gc]
            else:
                tap = _shift_rows(p_scr[g0 + a:g0 + a + gc + 1], r)
            term = wa_scr[k][None] * tap
            acc = term if acc is None else acc + term
        pa_scr[r0:r0 + rows, :] = (ab_ref[r0:r0 + rows, :].astype(F32) * acc.reshape(rows, d)).astype(BF16)

    merged = ga_ref[...].astype(F32) * _dot(pa_scr[...], woa_ref[...])
    merged = merged + gb_ref[...].astype(F32) * _dot(o_ref[...], wob_ref[...])
    merged = merged + gc_ref[...].astype(F32) * _dot(uc_scr[...], woc_ref[...])
    out_ref[...] = x_ref[...] + _dot(merged.astype(BF16), wo_ref[...])


def _mixer(x, z, o, conv_a, conv_c, conv_c_bias, ln_g, ln_b, w_out_a, w_out_b, w_out_c, w_o,
           layer, cols, *, seq, tm, rows=32):
    n, d = x.shape
    hb = tm // HALO
    cb = {name: c // d for name, c in cols.items()}

    def zspec(name):
        j = cb[name]
        return pl.BlockSpec((tm, d), lambda i: (i, j))

    def hspec(name):
        j = cb[name]
        return pl.BlockSpec((HALO, d), lambda i: (jnp.maximum(i * hb - 1, 0), j))

    def vec(k):
        return pl.BlockSpec((None, k, d), lambda i: (layer, 0, 0))

    wspec = pl.BlockSpec((None, d, d), lambda i: (layer, 0, 0))
    tile = pl.BlockSpec((tm, d), lambda i: (i, 0))
    return pl.pallas_call(
        functools.partial(_mixer_kernel, tm=tm, seq=seq, rows=rows),
        grid=(n // tm,),
        in_specs=[tile] + [zspec(k) for k in ("a_b", "a_c", "a_u", "c_val", "c_gate", "g_a", "g_b", "g_c")]
        + [hspec(k) for k in ("a_c", "a_u", "c_val", "c_gate")] + [tile]
        + [vec(CONV_A_WIDTH), vec(CONV_C_WIDTH), vec(1), vec(1), vec(1)]
        + [wspec] * 4,
        out_specs=tile,
        out_shape=jax.ShapeDtypeStruct((n, d), F32),
        scratch_shapes=[
            pltpu.VMEM((SUBLANES, (tm + HALO) // SUBLANES, SUBLANES, d), F32),
            pltpu.VMEM(((tm + HALO) // SUBLANES, SUBLANES, d), F32),
            pltpu.VMEM((tm, d), BF16),
            pltpu.VMEM((tm, d), BF16),
            pltpu.VMEM((CONV_C_WIDTH, SUBLANES, d), F32),
            pltpu.VMEM((CONV_A_WIDTH, SUBLANES, d), F32),
        ],
        compiler_params=_params("parallel", fuse=[False] * 19 + [True] * 4),
        name="mixer_merge",
    )(x, *([z] * 12), o, conv_a, conv_c, conv_c_bias, ln_g, ln_b, w_out_a, w_out_b, w_out_c, w_o)


def _mem_kv_kernel(mem_ref, g_ref, w_ref, kv_ref):
    kv_ref[...] = _dot(_rms_norm(mem_ref[...], g_ref[...]).astype(BF16), w_ref[...]).astype(kv_ref.dtype)


def _mem_kv(mem, gain, w_xkv):
    m, d = mem.shape
    depth, _, e = w_xkv.shape
    return pl.pallas_call(
        _mem_kv_kernel,
        grid=(depth,),
        in_specs=[
            pl.BlockSpec((m, d), lambda l: (0, 0)),
            pl.BlockSpec((1, d), lambda l: (0, 0)),
            pl.BlockSpec((None, d, e), lambda l: (l, 0, 0)),
        ],
        out_specs=pl.BlockSpec((None, m, e), lambda l: (l, 0, 0)),
        out_shape=jax.ShapeDtypeStruct((depth, m, e), BF16),
        compiler_params=_params("parallel", fuse=[False, False, True]),
        name="mem_kv",
    )(mem, gain, w_xkv)


def _xattn_kernel(x_ref, g_ref, wq_ref, kv_ref, wo_ref, out_ref):
    x = x_ref[...]
    q = _dot(_rms_norm(x, g_ref[...]).astype(BF16), wq_ref[...]) * (X_HEAD_DIM ** -0.5)
    q = q.astype(BF16)
    dx = X_HEADS * X_HEAD_DIM
    outs = []
    for h in range(X_HEADS):
        sl = slice(h * X_HEAD_DIM, (h + 1) * X_HEAD_DIM)
        s = _dot_nt(q[:, sl], kv_ref[:, sl])
        p = jnp.exp(s - jnp.max(s, axis=-1, keepdims=True))
        oh = _dot(p.astype(BF16), kv_ref[:, dx + h * X_HEAD_DIM:dx + (h + 1) * X_HEAD_DIM])
        outs.append((oh / jnp.sum(p, axis=-1, keepdims=True)).astype(BF16))
    out_ref[...] = x + _dot(jnp.concatenate(outs, axis=-1), wo_ref[...])


def _xattn(x, gain, w_xq, kv, w_xo, layer, *, seq, n_mem, tm):
    n, d = x.shape
    dx = w_xq.shape[-1]
    per_seq = seq // tm
    tile = pl.BlockSpec((tm, d), lambda i: (i, 0))
    return pl.pallas_call(
        _xattn_kernel,
        grid=(n // tm,),
        in_specs=[
            tile,
            pl.BlockSpec((None, 1, d), lambda i: (layer, 0, 0)),
            pl.BlockSpec((None, d, dx), lambda i: (layer, 0, 0)),
            pl.BlockSpec((None, None, n_mem, 2 * dx), lambda i: (layer, i // per_seq, 0, 0)),
            pl.BlockSpec((None, dx, d), lambda i: (layer, 0, 0)),
        ],
        out_specs=tile,
        out_shape=jax.ShapeDtypeStruct((n, d), F32),
        compiler_params=_params("parallel", fuse=[False, False, True, False, True]),
        name="xattn",
    )(x, gain, w_xq, kv, w_xo)


def _ffn_kernel(x_ref, g_ref, wg_ref, wu_ref, wd_ref, fg_ref, out_ref, *, final, chunks):
    x = x_ref[...]
    xn = _rms_norm(x, g_ref[...]).astype(BF16)
    y = x
    for c0, c1 in chunks:
        g = _dot(xn, wg_ref[:, c0:c1])
        h = (g * _sigmoid(g) * _dot(xn, wu_ref[:, c0:c1])).astype(BF16)
        y = y + _dot(h, wd_ref[c0:c1, :])
    if final:
        y = _rms_norm(y, fg_ref[...])
    out_ref[...] = y


def _ffn_chunks(dff):
    tiles = dff // MXU_WIDTH
    assert tiles * MXU_WIDTH == dff
    half = (tiles + 1) // 2
    return tuple((a * MXU_WIDTH, b * MXU_WIDTH) for a, b in ((0, half), (half, tiles)) if b > a)


def _ffn(x, gain, w_gate_up, w_down, final_gain, layer, *, tm, final):
    n, d = x.shape
    dff = w_down.shape[1]
    tile = pl.BlockSpec((tm, d), lambda i: (i, 0))
    once = pl.Buffered(1)
    return pl.pallas_call(
        functools.partial(_ffn_kernel, final=final, chunks=_ffn_chunks(dff)),
        grid=(n // tm,),
        in_specs=[
            tile,
            pl.BlockSpec((None, 1, d), lambda i: (layer, 0, 0)),
            pl.BlockSpec((None, d, dff), lambda i: (layer, 0, 0), pipeline_mode=once),
            pl.BlockSpec((None, d, dff), lambda i: (layer, 0, 1), pipeline_mode=once),
            pl.BlockSpec((None, dff, d), lambda i: (layer, 0, 0), pipeline_mode=once),
            pl.BlockSpec((1, d), lambda i: (0, 0)),
        ],
        out_specs=tile,
        out_shape=jax.ShapeDtypeStruct((n, d), F32),
        compiler_params=_params("parallel", fuse=[False, False, True, True, True, False]),
        name="ffn",
    )(x, gain, w_gate_up, w_gate_up, w_down, final_gain)


def _pick(n, pref):
    t = min(n, pref)
    assert n % t == 0, (n, t)
    return t


def kernel(x, mem, mix_norm, w_in, b_gate, b_forget, conv_a, w_out_a, w_out_b, conv_c, conv_c_bias,
           ln_c_gain, ln_c_bias, w_out_c, w_o, xattn_norm, mem_norm, w_xq, w_xkv, w_xo, ffn_norm,
           w_gate_up, w_down, final_norm):
    batch, seq, d = x.shape
    depth = w_in.shape[0]
    n_mem = mem.shape[1]
    n = batch * seq
    d_b = FOX_HEADS * FOX_HEAD_DIM
    assert d % LANES == 0 and seq % CUM_BLOCK == 0 and d_b == d

    names = ("a_b", "a_c", "a_u", "q", "k", "v", "c_val", "c_gate", "g_a", "g_b", "g_c")
    cols = {name: idx * d for idx, name in enumerate(names)}
    f_col = 6 * d
    w_in16 = w_in.astype(BF16)
    w_main = jnp.concatenate([w_in16[:, :, :f_col], w_in16[:, :, f_col + FOX_HEADS:]], axis=-1)
    e_main = w_main.shape[-1]
    w_f = jnp.pad(w_in16[:, :, f_col:f_col + FOX_HEADS], ((0, 0), (0, 0), (0, LANES - FOX_HEADS)))
    b_f = jnp.pad(b_forget, ((0, 0), (0, LANES - FOX_HEADS)))[:, None, :]
    tri = jnp.tril(jnp.ones((CUM_BLOCK, CUM_BLOCK), BF16))
    col_scale = jnp.ones((1, e_main), F32).at[:, cols["q"]:cols["q"] + d_b].set(
        FOX_HEAD_DIM ** -0.5 * LOG2E)

    row3 = lambda a: a[:, None, :]
    bf = lambda a: a.astype(BF16)
    gate_col = cols["c_gate"]
    col_bias = jnp.zeros((depth, 1, e_main), F32).at[:, 0, cols["g_a"]:].set(b_gate)

    kv = _mem_kv(mem.reshape(batch * n_mem, d), mem_norm[None, :], bf(w_xkv))
    kv = kv.reshape(depth, batch, n_mem, kv.shape[-1])
    w_oa, w_ob, w_oc, w_oo = bf(w_out_a), bf(w_out_b), bf(w_out_c), bf(w_o)
    w_q, w_xo_b, w_gu, w_dn = bf(w_xq), bf(w_xo), bf(w_gate_up), bf(w_down)

    xs = x.reshape(n, d)
    for l in range(depth):
        z, f = _in_proj(xs, row3(mix_norm), w_main, w_f, col_scale, col_bias, l, tm=_pick(n, 2048),
                        tn=_pick(d, 1024), gate_col=gate_col)
        c_row = _forget_cumsum(f, b_f, l, tri, batch=batch, seq=seq)
        c_row = c_row.reshape(batch, FOX_HEADS // 2, 2, seq)
        o = _fox_attention(z, c_row, batch=batch, seq=seq, q_col=cols["q"], k_col=cols["k"],
                           v_col=cols["v"], blk=_pick(seq, 512))
        xs = _mixer(xs, z, o, conv_a, conv_c, row3(conv_c_bias), row3(ln_c_gain), row3(ln_c_bias),
                    w_oa, w_ob, w_oc, w_oo, l, cols, seq=seq, tm=_pick(seq, 256))
        xs = _xattn(xs, row3(xattn_norm), w_q, kv, w_xo_b, l, seq=seq, n_mem=n_mem, tm=_pick(seq, 1024))
        xs = _ffn(xs, row3(ffn_norm), w_gu, w_dn, final_norm[None, :], l, tm=_pick(n, 512),
                  final=(l == depth - 1))
    return xs.reshape(batch, seq, d)
```

```python
import functools

import jax
import jax.numpy as jnp
from jax import lax
from jax.experimental import pallas as pl
from jax.experimental.pallas import tpu as pltpu

EPS = 1e-6
FOX_HEADS = 16
FOX_HEAD_DIM = 64
X_HEADS = 4
X_HEAD_DIM = 128
CONV_A_WIDTH = 3
CONV_C_WIDTH = 31
LANES = 128
HALO = 32
MXU_WIDTH = 256
CUM_BLOCK = MXU_WIDTH
VMEM_LIMIT = 56 * 1024 * 1024
SUBLANES = 8
LOG2E = 1.4426950408889634

F32 = jnp.float32
BF16 = jnp.bfloat16


def _params(*sem, fuse=None):
    return pltpu.CompilerParams(dimension_semantics=sem, vmem_limit_bytes=VMEM_LIMIT, allow_input_fusion=fuse)


def _rms_norm(x, g):
    return x * lax.rsqrt(jnp.mean(x * x, axis=-1, keepdims=True) + EPS) * g


def _sigmoid(x):
    return 0.5 * jnp.tanh(0.5 * x) + 0.5


def _dot(a, b):
    return jnp.dot(a, b, preferred_element_type=F32)


def _dot_nt(a, b):
    return lax.dot_general(a, b, (((1,), (1,)), ((), ())), preferred_element_type=F32)


def _in_proj_kernel(x_ref, g_ref, w_ref, wf_ref, cs_ref, cb_ref, z_ref, f_ref, xn_ref, *, gate_tile):
    j = pl.program_id(1)

    @pl.when(j == 0)
    def _():
        xn = _rms_norm(x_ref[...], g_ref[...]).astype(BF16)
        xn_ref[...] = xn
        f_ref[...] = _dot(xn, wf_ref[...])

    @pl.when(j < gate_tile)
    def _():
        z_ref[...] = (_dot(xn_ref[...], w_ref[...]) * cs_ref[...]).astype(z_ref.dtype)

    @pl.when(j >= gate_tile)
    def _():
        z_ref[...] = _sigmoid(_dot(xn_ref[...], w_ref[...]) + cb_ref[...]).astype(z_ref.dtype)


def _in_proj(x, gain, w_main, w_f, col_scale, col_bias, layer, *, tm, tn, gate_col):
    n, d = x.shape
    e = w_main.shape[-1]
    return pl.pallas_call(
        functools.partial(_in_proj_kernel, gate_tile=gate_col // tn),
        grid=(n // tm, e // tn),
        in_specs=[
            pl.BlockSpec((tm, d), lambda i, j: (i, 0)),
            pl.BlockSpec((None, 1, d), lambda i, j: (layer, 0, 0)),
            pl.BlockSpec((None, d, tn), lambda i, j: (layer, 0, j)),
            pl.BlockSpec((None, d, LANES), lambda i, j: (layer, 0, 0)),
            pl.BlockSpec((1, tn), lambda i, j: (0, j)),
            pl.BlockSpec((None, 1, tn), lambda i, j: (layer, 0, j)),
        ],
        out_specs=[
            pl.BlockSpec((tm, tn), lambda i, j: (i, j)),
            pl.BlockSpec((tm, LANES), lambda i, j: (i, 0)),
        ],
        out_shape=[
            jax.ShapeDtypeStruct((n, e), BF16),
            jax.ShapeDtypeStruct((n, LANES), F32),
        ],
        scratch_shapes=[pltpu.VMEM((tm, d), BF16)],
        compiler_params=_params("parallel", "arbitrary", fuse=[False, False, True, True, False, False]),
        name="in_proj",
    )(x, gain, w_main, w_f, col_scale, col_bias)


def _forget_cumsum_kernel(f_ref, b_ref, tri_ref, crow_ref, c_scr):
    x = f_ref[...] + b_ref[...]
    lf = jnp.minimum(x, 0.0) - jnp.log(1.0 + jnp.exp(-jnp.abs(x)))
    hi = lf.astype(BF16)
    rem = lf - hi.astype(F32)
    mid = rem.astype(BF16)
    lo = (rem - mid.astype(F32)).astype(BF16)
    tri = tri_ref[...]
    seq = f_ref.shape[0]
    carry = jnp.zeros((1, LANES), F32)
    for blk in range(seq // CUM_BLOCK):
        sl = slice(blk * CUM_BLOCK, (blk + 1) * CUM_BLOCK)
        part = _dot(tri, hi[sl]) + _dot(tri, mid[sl]) + _dot(tri, lo[sl]) + carry
        c_scr[sl, :] = part
        carry = part[CUM_BLOCK - 1:CUM_BLOCK, :]
    crow_ref[...] = c_scr[...].T[:FOX_HEADS, :]


def _forget_cumsum(f, b_forget, layer, tri, *, batch, seq):
    return pl.pallas_call(
        _forget_cumsum_kernel,
        grid=(batch,),
        in_specs=[
            pl.BlockSpec((seq, LANES), lambda b: (b, 0)),
            pl.BlockSpec((None, 1, LANES), lambda b: (layer, 0, 0)),
            pl.BlockSpec((CUM_BLOCK, CUM_BLOCK), lambda b: (0, 0)),
        ],
        out_specs=pl.BlockSpec((None, FOX_HEADS, seq), lambda b: (b, 0, 0)),
        out_shape=jax.ShapeDtypeStruct((batch, FOX_HEADS, seq), F32),
        scratch_shapes=[pltpu.VMEM((seq, LANES), F32)],
        compiler_params=_params("parallel"),
        name="forget_cumsum",
    )(f, b_forget, tri)


def _fox_kernel(q_ref, k_ref, v_ref, crow_ref, o_ref, *, blk):
    lane = lax.broadcasted_iota(jnp.int32, (1, LANES), 1)
    seq = k_ref.shape[0]

    def online(state, s, vblk):
        m, acc = state
        m_new = jnp.maximum(m, jnp.max(s, axis=-1, keepdims=True))
        p = jnp.exp2(s - m_new)
        return m_new, jnp.exp2(m - m_new) * acc + _dot(p.astype(BF16), vblk)

    def head_pair(pair, carry):
        lanes = pl.ds(pl.multiple_of(pair * LANES, LANES), LANES)
        for i in range(seq // blk):
            pair_block(pair, lanes, i)
        return carry

    def pair_block(pair, lanes, i):
        q = q_ref[i * blk:(i + 1) * blk, lanes]
        zero = jnp.zeros_like(q)
        qs = (jnp.where(lane < FOX_HEAD_DIM, q, zero), jnp.where(lane < FOX_HEAD_DIM, zero, q))
        init = (jnp.full((blk, 1), -jnp.inf, F32), jnp.zeros((blk, LANES), F32))
        state = [init, init]
        for j in range(i + 1):
            keys = slice(j * blk, (j + 1) * blk)
            v = v_ref[keys, lanes]
            ones = jnp.ones_like(v)
            vs = (jnp.where(lane < FOX_HEAD_DIM, v, ones), jnp.where(lane < FOX_HEAD_DIM, ones, v))
            for h in range(2):
                s = _dot_nt(qs[h], k_ref[keys, lanes]) - crow_ref[pair, h:h + 1, keys] * LOG2E
                if j == i:
                    row = lax.broadcasted_iota(jnp.int32, (blk, blk), 0)
                    col = lax.broadcasted_iota(jnp.int32, (blk, blk), 1)
                    s = jnp.where(row >= col, s, -jnp.inf)
                state[h] = online(state[h], s, vs[h])
        acc0, acc1 = state[0][1], state[1][1]
        num = jnp.where(lane < FOX_HEAD_DIM, acc0, acc1)
        den = jnp.where(lane < FOX_HEAD_DIM, pltpu.roll(acc0, FOX_HEAD_DIM, axis=1), pltpu.roll(acc1, FOX_HEAD_DIM, axis=1))
        o_ref[i * blk:(i + 1) * blk, lanes] = (num / den).astype(o_ref.dtype)

    lax.fori_loop(0, crow_ref.shape[0], head_pair, 0)


def _fox_attention(z, c_row, *, batch, seq, q_col, k_col, v_col, blk):
    n = z.shape[0]
    pairs = FOX_HEADS // 2
    width = pairs * LANES

    def cols(first):
        return pl.BlockSpec((seq, width), lambda b: (b, first // width))

    return pl.pallas_call(
        functools.partial(_fox_kernel, blk=blk),
        grid=(batch,),
        in_specs=[cols(q_col), cols(k_col), cols(v_col), pl.BlockSpec((None, pairs, 2, seq), lambda b: (b, 0, 0, 0))],
        out_specs=cols(0),
        out_shape=jax.ShapeDtypeStruct((n, width), BF16),
        compiler_params=_params("parallel"),
        name="fox_attention",
    )(z, z, z, c_row)


def _shift_rows(x3, r):
    sub = lax.broadcasted_iota(jnp.int32, x3.shape[1:], 0)[None]
    rolled = pltpu.roll(x3, SUBLANES - r, axis=1)
    return jnp.where(sub < SUBLANES - r, rolled[:-1], rolled[1:])


def _mixer_kernel(x_ref, ab_ref, ac_ref, au_ref, cv_ref, cg_ref, ga_ref, gb_ref, gc_ref,
                  hac_ref, hau_ref, hcv_ref, hcg_ref, o_ref,
                  conva_ref, convc_ref, cbias_ref, lng_ref, lnb_ref,
                  woa_ref, wob_ref, woc_ref, wo_ref, out_ref,
                  ush_scr, p_scr, uc_scr, pa_scr, wc_scr, wa_scr, *, tm, seq, rows):
    d = x_ref.shape[-1]
    hg = HALO // SUBLANES
    gu = hg + tm // SUBLANES
    gc = rows // SUBLANES

    def grouped(v):
        return v.reshape(v.shape[0] // SUBLANES, SUBLANES, d)

    for k in range(CONV_C_WIDTH):
        wc_scr[k] = jnp.broadcast_to(convc_ref[k:k + 1, :], (SUBLANES, d))
    for k in range(CONV_A_WIDTH):
        wa_scr[k] = jnp.broadcast_to(conva_ref[k:k + 1, :], (SUBLANES, d))

    keep = jnp.where((pl.program_id(0) * tm) % seq == 0, 0.0, 1.0)
    ush_scr[0, 0:hg] = grouped(hcv_ref[...].astype(F32) * hcg_ref[...].astype(F32) * keep)
    ush_scr[0, hg:] = grouped(cv_ref[...].astype(F32) * cg_ref[...].astype(F32))
    p_scr[0:hg] = grouped(hac_ref[...].astype(F32) * hau_ref[...].astype(F32) * keep)
    p_scr[hg:] = grouped(ac_ref[...].astype(F32) * au_ref[...].astype(F32))
    u0 = ush_scr[0]
    for r in range(1, SUBLANES):
        ush_scr[r, 0:gu - 1] = _shift_rows(u0, r)

    for c in range(tm // rows):
        g0 = c * gc
        r0 = c * rows
        acc = jnp.broadcast_to(cbias_ref[...][None], (gc, SUBLANES, d))
        for k in range(CONV_C_WIDTH):
            a, r = divmod(HALO - (CONV_C_WIDTH - 1) + k, SUBLANES)
            acc = acc + wc_scr[k][None] * ush_scr[r, g0 + a:g0 + a + gc]
        mu = jnp.mean(acc, axis=-1, keepdims=True)
        xc = acc - mu
        y = xc * lax.rsqrt(jnp.mean(xc * xc, axis=-1, keepdims=True) + EPS) * lng_ref[...][None] + lnb_ref[...][None]
        uc_scr[r0:r0 + rows, :] = (y * _sigmoid(y)).reshape(rows, d).astype(BF16)
        acc = None
        for k in range(CONV_A_WIDTH):
            a, r = divmod(HALO - (CONV_A_WIDTH - 1) + k, SUBLANES)
            if r == 0:
                tap = p_scr[g0 + a:g0 + a + gc]
            else:
                tap = _shift_rows(p_scr[g0 + a:g0 + a + gc + 1], r)
            term = wa_scr[k][None] * tap
            acc = term if acc is None else acc + term
        pa_scr[r0:r0 + rows, :] = (ab_ref[r0:r0 + rows, :].astype(F32) * acc.reshape(rows, d)).astype(BF16)

    merged = ga_ref[...].astype(F32) * _dot(pa_scr[...], woa_ref[...])
    merged = merged + gb_ref[...].astype(F32) * _dot(o_ref[...], wob_ref[...])
    merged = merged + gc_ref[...].astype(F32) * _dot(uc_scr[...], woc_ref[...])
    out_ref[...] = x_ref[...] + _dot(merged.astype(BF16), wo_ref[...])


def _mixer(x, z, o, conv_a, conv_c, conv_c_bias, ln_g, ln_b, w_out_a, w_out_b, w_out_c, w_o,
           layer, cols, *, seq, tm, rows=32):
    n, d = x.shape
    hb = tm // HALO
    cb = {name: c // d for name, c in cols.items()}

    def zspec(name):
        j = cb[name]
        return pl.BlockSpec((tm, d), lambda i: (i, j))

    def hspec(name):
        j = cb[name]
        return pl.BlockSpec((HALO, d), lambda i: (jnp.maximum(i * hb - 1, 0), j))

    def vec(k):
        return pl.BlockSpec((None, k, d), lambda i: (layer, 0, 0))

    wspec = pl.BlockSpec((None, d, d), lambda i: (layer, 0, 0))
    tile = pl.BlockSpec((tm, d), lambda i: (i, 0))
    return pl.pallas_call(
        functools.partial(_mixer_kernel, tm=tm, seq=seq, rows=rows),
        grid=(n // tm,),
        in_specs=[tile] + [zspec(k) for k in ("a_b", "a_c", "a_u", "c_val", "c_gate", "g_a", "g_b", "g_c")]
        + [hspec(k) for k in ("a_c", "a_u", "c_val", "c_gate")] + [tile]
        + [vec(CONV_A_WIDTH), vec(CONV_C_WIDTH), vec(1), vec(1), vec(1)]
        + [wspec] * 4,
        out_specs=tile,
        out_shape=jax.ShapeDtypeStruct((n, d), F32),
        scratch_shapes=[
            pltpu.VMEM((SUBLANES, (tm + HALO) // SUBLANES, SUBLANES, d), F32),
            pltpu.VMEM(((tm + HALO) // SUBLANES, SUBLANES, d), F32),
            pltpu.VMEM((tm, d), BF16),
            pltpu.VMEM((tm, d), BF16),
            pltpu.VMEM((CONV_C_WIDTH, SUBLANES, d), F32),
            pltpu.VMEM((CONV_A_WIDTH, SUBLANES, d), F32),
        ],
        compiler_params=_params("parallel", fuse=[False] * 19 + [True] * 4),
        name="mixer_merge",
    )(x, *([z] * 12), o, conv_a, conv_c, conv_c_bias, ln_g, ln_b, w_out_a, w_out_b, w_out_c, w_o)


def _mem_kv_kernel(mem_ref, g_ref, w_ref, kv_ref):
    kv_ref[...] = _dot(_rms_norm(mem_ref[...], g_ref[...]).astype(BF16), w_ref[...]).astype(kv_ref.dtype)


def _mem_kv(mem, gain, w_xkv):
    m, d = mem.shape
    depth, _, e = w_xkv.shape
    return pl.pallas_call(
        _mem_kv_kernel,
        grid=(depth,),
        in_specs=[
            pl.BlockSpec((m, d), lambda l: (0, 0)),
            pl.BlockSpec((1, d), lambda l: (0, 0)),
            pl.BlockSpec((None, d, e), lambda l: (l, 0, 0)),
        ],
        out_specs=pl.BlockSpec((None, m, e), lambda l: (l, 0, 0)),
        out_shape=jax.ShapeDtypeStruct((depth, m, e), BF16),
        compiler_params=_params("parallel", fuse=[False, False, True]),
        name="mem_kv",
    )(mem, gain, w_xkv)


def _xattn_kernel(x_ref, g_ref, wq_ref, kv_ref, wo_ref, out_ref):
    x = x_ref[...]
    q = _dot(_rms_norm(x, g_ref[...]).astype(BF16), wq_ref[...]) * (X_HEAD_DIM ** -0.5)
    q = q.astype(BF16)
    dx = X_HEADS * X_HEAD_DIM
    outs = []
    for h in range(X_HEADS):
        sl = slice(h * X_HEAD_DIM, (h + 1) * X_HEAD_DIM)
        s = _dot_nt(q[:, sl], kv_ref[:, sl])
        p = jnp.exp(s - jnp.max(s, axis=-1, keepdims=True))
        oh = _dot(p.astype(BF16), kv_ref[:, dx + h * X_HEAD_DIM:dx + (h + 1) * X_HEAD_DIM])
        outs.append((oh / jnp.sum(p, axis=-1, keepdims=True)).astype(BF16))
    out_ref[...] = x + _dot(jnp.concatenate(outs, axis=-1), wo_ref[...])


def _xattn(x, gain, w_xq, kv, w_xo, layer, *, seq, n_mem, tm):
    n, d = x.shape
    dx = w_xq.shape[-1]
    per_seq = seq // tm
    tile = pl.BlockSpec((tm, d), lambda i: (i, 0))
    return pl.pallas_call(
        _xattn_kernel,
        grid=(n // tm,),
        in_specs=[
            tile,
            pl.BlockSpec((None, 1, d), lambda i: (layer, 0, 0)),
            pl.BlockSpec((None, d, dx), lambda i: (layer, 0, 0)),
            pl.BlockSpec((None, None, n_mem, 2 * dx), lambda i: (layer, i // per_seq, 0, 0)),
            pl.BlockSpec((None, dx, d), lambda i: (layer, 0, 0)),
        ],
        out_specs=tile,
        out_shape=jax.ShapeDtypeStruct((n, d), F32),
        compiler_params=_params("parallel", fuse=[False, False, True, False, True]),
        name="xattn",
    )(x, gain, w_xq, kv, w_xo)


def _ffn_kernel(x_ref, g_ref, wg_ref, wu_ref, wd_ref, fg_ref, out_ref, *, final, chunks):
    x = x_ref[...]
    xn = _rms_norm(x, g_ref[...]).astype(BF16)
    y = x
    for c0, c1 in chunks:
        g = _dot(xn, wg_ref[:, c0:c1])
        h = (g * _sigmoid(g) * _dot(xn, wu_ref[:, c0:c1])).astype(BF16)
        y = y + _dot(h, wd_ref[c0:c1, :])
    if final:
        y = _rms_norm(y, fg_ref[...])
    out_ref[...] = y


def _ffn_chunks(dff):
    tiles = dff // MXU_WIDTH
    assert tiles * MXU_WIDTH == dff
    half = (tiles + 1) // 2
    return tuple((a * MXU_WIDTH, b * MXU_WIDTH) for a, b in ((0, half), (half, tiles)) if b > a)


def _ffn(x, gain, w_gate_up, w_down, final_gain, layer, *, tm, final):
    n, d = x.shape
    dff = w_down.shape[1]
    tile = pl.BlockSpec((tm, d), lambda i: (i, 0))
    once = pl.Buffered(1)
    return pl.pallas_call(
        functools.partial(_ffn_kernel, final=final, chunks=_ffn_chunks(dff)),
        grid=(n // tm,),
        in_specs=[
            tile,
            pl.BlockSpec((None, 1, d), lambda i: (layer, 0, 0)),
            pl.BlockSpec((None, d, dff), lambda i: (layer, 0, 0), pipeline_mode=once),
            pl.BlockSpec((None, d, dff), lambda i: (layer, 0, 1), pipeline_mode=once),
            pl.BlockSpec((None, dff, d), lambda i: (layer, 0, 0), pipeline_mode=once),
            pl.BlockSpec((1, d), lambda i: (0, 0)),
        ],
        out_specs=tile,
        out_shape=jax.ShapeDtypeStruct((n, d), F32),
        compiler_params=_params("parallel", fuse=[False, False, True, True, True, False]),
        name="ffn",
    )(x, gain, w_gate_up, w_gate_up, w_down, final_gain)


def _pick(n, pref):
    t = min(n, pref)
    assert n % t == 0, (n, t)
    return t


def kernel(x, mem, mix_norm, w_in, b_gate, b_forget, conv_a, w_out_a, w_out_b, conv_c, conv_c_bias,
           ln_c_gain, ln_c_bias, w_out_c, w_o, xattn_norm, mem_norm, w_xq, w_xkv, w_xo, ffn_norm,
           w_gate_up, w_down, final_norm):
    batch, seq, d = x.shape
    depth = w_in.shape[0]
    n_mem = mem.shape[1]
    n = batch * seq
    d_b = FOX_HEADS * FOX_HEAD_DIM
    assert d % LANES == 0 and seq % CUM_BLOCK == 0 and d_b == d

    names = ("a_b", "a_c", "a_u", "q", "k", "v", "c_val", "c_gate", "g_a", "g_b", "g_c")
    cols = {name: idx * d for idx, name in enumerate(names)}
    f_col = 6 * d
    w_in16 = w_in.astype(BF16)
    w_main = jnp.concatenate([w_in16[:, :, :f_col], w_in16[:, :, f_col + FOX_HEADS:]], axis=-1)
    e_main = w_main.shape[-1]
    w_f = jnp.pad(w_in16[:, :, f_col:f_col + FOX_HEADS], ((0, 0), (0, 0), (0, LANES - FOX_HEADS)))
    b_f = jnp.pad(b_forget, ((0, 0), (0, LANES - FOX_HEADS)))[:, None, :]
    tri = jnp.tril(jnp.ones((CUM_BLOCK, CUM_BLOCK), BF16))
    col_scale = jnp.ones((1, e_main), F32).at[:, cols["q"]:cols["q"] + d_b].set(
        FOX_HEAD_DIM ** -0.5 * LOG2E)

    row3 = lambda a: a[:, None, :]
    bf = lambda a: a.astype(BF16)
    gate_col = cols["c_gate"]
    col_bias = jnp.zeros((depth, 1, e_main), F32).at[:, 0, cols["g_a"]:].set(b_gate)

    kv = _mem_kv(mem.reshape(batch * n_mem, d), mem_norm[None, :], bf(w_xkv))
    kv = kv.reshape(depth, batch, n_mem, kv.shape[-1])
    w_oa, w_ob, w_oc, w_oo = bf(w_out_a), bf(w_out_b), bf(w_out_c), bf(w_o)
    w_q, w_xo_b, w_gu, w_dn = bf(w_xq), bf(w_xo), bf(w_gate_up), bf(w_down)

    xs = x.reshape(n, d)
    for l in range(depth):
        z, f = _in_proj(xs, row3(mix_norm), w_main, w_f, col_scale, col_bias, l, tm=_pick(n, 2048),
                        tn=_pick(d, 1024), gate_col=gate_col)
        c_row = _forget_cumsum(f, b_f, l, tri, batch=batch, seq=seq)
        c_row = c_row.reshape(batch, FOX_HEADS // 2, 2, seq)
        o = _fox_attention(z, c_row, batch=batch, seq=seq, q_col=cols["q"], k_col=cols["k"],
                           v_col=cols["v"], blk=_pick(seq, 512))
        xs = _mixer(xs, z, o, conv_a, conv_c, row3(conv_c_bias), row3(ln_c_gain), row3(ln_c_bias),
                    w_oa, w_ob, w_oc, w_oo, l, cols, seq=seq, tm=_pick(seq, 256))
        xs = _xattn(xs, row3(xattn_norm), w_q, kv, w_xo_b, l, seq=seq, n_mem=n_mem, tm=_pick(seq, 1024))
        xs = _ffn(xs, row3(ffn_norm), w_gu, w_dn, final_norm[None, :], l, tm=_pick(n, 1024),
                  final=(l == depth - 1))
    return xs.reshape(batch, seq, d)
```

```python
import functools

import jax
import jax.numpy as jnp
from jax import lax
from jax.experimental import pallas as pl
from jax.experimental.pallas import tpu as pltpu

EPS = 1e-6
FOX_HEADS = 16
FOX_HEAD_DIM = 64
X_HEADS = 4
X_HEAD_DIM = 128
CONV_A_WIDTH = 3
CONV_C_WIDTH = 31
LANES = 128
HALO = 32
MXU_WIDTH = 256
CUM_BLOCK = MXU_WIDTH
VMEM_LIMIT = 56 * 1024 * 1024
SUBLANES = 8
LOG2E = 1.4426950408889634

F32 = jnp.float32
BF16 = jnp.bfloat16


def _params(*sem, fuse=None):
    return pltpu.CompilerParams(dimension_semantics=sem, vmem_limit_bytes=VMEM_LIMIT, allow_input_fusion=fuse)


def _rms_norm(x, g):
    return x * lax.rsqrt(jnp.mean(x * x, axis=-1, keepdims=True) + EPS) * g


def _sigmoid(x):
    return 0.5 * jnp.tanh(0.5 * x) + 0.5


def _dot(a, b):
    return jnp.dot(a, b, preferred_element_type=F32)


def _dot_nt(a, b):
    return lax.dot_general(a, b, (((1,), (1,)), ((), ())), preferred_element_type=F32)


def _in_proj_kernel(x_ref, g_ref, w_ref, wf_ref, cs_ref, cb_ref, z_ref, f_ref, xn_ref, *, gate_tile):
    j = pl.program_id(1)

    @pl.when(j == 0)
    def _():
        xn = _rms_norm(x_ref[...], g_ref[...]).astype(BF16)
        xn_ref[...] = xn
        f_ref[...] = _dot(xn, wf_ref[...])

    @pl.when(j < gate_tile)
    def _():
        z_ref[...] = (_dot(xn_ref[...], w_ref[...]) * cs_ref[...]).astype(z_ref.dtype)

    @pl.when(j >= gate_tile)
    def _():
        z_ref[...] = _sigmoid(_dot(xn_ref[...], w_ref[...]) + cb_ref[...]).astype(z_ref.dtype)


def _in_proj(x, gain, w_main, w_f, col_scale, col_bias, layer, *, tm, tn, gate_col):
    n, d = x.shape
    e = w_main.shape[-1]
    return pl.pallas_call(
        functools.partial(_in_proj_kernel, gate_tile=gate_col // tn),
        grid=(n // tm, e // tn),
        in_specs=[
            pl.BlockSpec((tm, d), lambda i, j: (i, 0)),
            pl.BlockSpec((None, 1, d), lambda i, j: (layer, 0, 0)),
            pl.BlockSpec((None, d, tn), lambda i, j: (layer, 0, j)),
            pl.BlockSpec((None, d, LANES), lambda i, j: (layer, 0, 0)),
            pl.BlockSpec((1, tn), lambda i, j: (0, j)),
            pl.BlockSpec((None, 1, tn), lambda i, j: (layer, 0, j)),
        ],
        out_specs=[
            pl.BlockSpec((tm, tn), lambda i, j: (i, j)),
            pl.BlockSpec((tm, LANES), lambda i, j: (i, 0)),
        ],
        out_shape=[
            jax.ShapeDtypeStruct((n, e), BF16),
            jax.ShapeDtypeStruct((n, LANES), F32),
        ],
        scratch_shapes=[pltpu.VMEM((tm, d), BF16)],
        compiler_params=_params("parallel", "arbitrary", fuse=[False, False, True, True, False, False]),
        name="in_proj",
    )(x, gain, w_main, w_f, col_scale, col_bias)


def _forget_cumsum_kernel(f_ref, b_ref, tri_ref, crow_ref, c_scr):
    x = f_ref[...] + b_ref[...]
    lf = jnp.minimum(x, 0.0) - jnp.log(1.0 + jnp.exp(-jnp.abs(x)))
    hi = lf.astype(BF16)
    rem = lf - hi.astype(F32)
    mid = rem.astype(BF16)
    lo = (rem - mid.astype(F32)).astype(BF16)
    tri = tri_ref[...]
    seq = f_ref.shape[0]
    carry = jnp.zeros((1, LANES), F32)
    for blk in range(seq // CUM_BLOCK):
        sl = slice(blk * CUM_BLOCK, (blk + 1) * CUM_BLOCK)
        part = _dot(tri, hi[sl]) + _dot(tri, mid[sl]) + _dot(tri, lo[sl]) + carry
        c_scr[sl, :] = part
        carry = part[CUM_BLOCK - 1:CUM_BLOCK, :]
    crow_ref[...] = c_scr[...].T[:FOX_HEADS, :]


def _forget_cumsum(f, b_forget, layer, tri, *, batch, seq):
    return pl.pallas_call(
        _forget_cumsum_kernel,
        grid=(batch,),
        in_specs=[
            pl.BlockSpec((seq, LANES), lambda b: (b, 0)),
            pl.BlockSpec((None, 1, LANES), lambda b: (layer, 0, 0)),
            pl.BlockSpec((CUM_BLOCK, CUM_BLOCK), lambda b: (0, 0)),
        ],
        out_specs=pl.BlockSpec((None, FOX_HEADS, seq), lambda b: (b, 0, 0)),
        out_shape=jax.ShapeDtypeStruct((batch, FOX_HEADS, seq), F32),
        scratch_shapes=[pltpu.VMEM((seq, LANES), F32)],
        compiler_params=_params("parallel"),
        name="forget_cumsum",
    )(f, b_forget, tri)


def _fox_kernel(q_ref, k_ref, v_ref, crow_ref, o_ref, *, blk):
    lane = lax.broadcasted_iota(jnp.int32, (1, LANES), 1)
    seq = k_ref.shape[0]

    def online(state, s, vblk):
        m, acc = state
        m_new = jnp.maximum(m, jnp.max(s, axis=-1, keepdims=True))
        p = jnp.exp2(s - m_new)
        return m_new, jnp.exp2(m - m_new) * acc + _dot(p.astype(BF16), vblk)

    def head_pair(pair, carry):
        lanes = pl.ds(pl.multiple_of(pair * LANES, LANES), LANES)
        for i in range(seq // blk):
            pair_block(pair, lanes, i)
        return carry

    def pair_block(pair, lanes, i):
        q = q_ref[i * blk:(i + 1) * blk, lanes]
        zero = jnp.zeros_like(q)
        qs = (jnp.where(lane < FOX_HEAD_DIM, q, zero), jnp.where(lane < FOX_HEAD_DIM, zero, q))
        init = (jnp.full((blk, 1), -jnp.inf, F32), jnp.zeros((blk, LANES), F32))
        state = [init, init]
        for j in range(i + 1):
            keys = slice(j * blk, (j + 1) * blk)
            v = v_ref[keys, lanes]
            ones = jnp.ones_like(v)
            vs = (jnp.where(lane < FOX_HEAD_DIM, v, ones), jnp.where(lane < FOX_HEAD_DIM, ones, v))
            for h in range(2):
                s = _dot_nt(qs[h], k_ref[keys, lanes]) - crow_ref[pair, h:h + 1, keys] * LOG2E
                if j == i:
                    row = lax.broadcasted_iota(jnp.int32, (blk, blk), 0)
                    col = lax.broadcasted_iota(jnp.int32, (blk, blk), 1)
                    s = jnp.where(row >= col, s, -jnp.inf)
                state[h] = online(state[h], s, vs[h])
        acc0, acc1 = state[0][1], state[1][1]
        num = jnp.where(lane < FOX_HEAD_DIM, acc0, acc1)
        den = jnp.where(lane < FOX_HEAD_DIM, pltpu.roll(acc0, FOX_HEAD_DIM, axis=1), pltpu.roll(acc1, FOX_HEAD_DIM, axis=1))
        o_ref[i * blk:(i + 1) * blk, lanes] = (num / den).astype(o_ref.dtype)

    lax.fori_loop(0, crow_ref.shape[0], head_pair, 0)


def _fox_attention(z, c_row, *, batch, seq, q_col, k_col, v_col, blk):
    n = z.shape[0]
    pairs = FOX_HEADS // 2
    width = pairs * LANES

    def cols(first):
        return pl.BlockSpec((seq, width), lambda b: (b, first // width))

    return pl.pallas_call(
        functools.partial(_fox_kernel, blk=blk),
        grid=(batch,),
        in_specs=[cols(q_col), cols(k_col), cols(v_col), pl.BlockSpec((None, pairs, 2, seq), lambda b: (b, 0, 0, 0))],
        out_specs=cols(0),
        out_shape=jax.ShapeDtypeStruct((n, width), BF16),
        compiler_params=_params("parallel"),
        name="fox_attention",
    )(z, z, z, c_row)


def _shift_rows(x3, r):
    sub = lax.broadcasted_iota(jnp.int32, x3.shape[1:], 0)[None]
    rolled = pltpu.roll(x3, SUBLANES - r, axis=1)
    return jnp.where(sub < SUBLANES - r, rolled[:-1], rolled[1:])


def _mixer_kernel(x_ref, ab_ref, ac_ref, au_ref, cv_ref, cg_ref, ga_ref, gb_ref, gc_ref,
                  hac_ref, hau_ref, hcv_ref, hcg_ref, o_ref,
                  conva_ref, convc_ref, cbias_ref, lng_ref, lnb_ref,
                  woa_ref, wob_ref, woc_ref, wo_ref, out_ref,
                  ush_scr, p_scr, uc_scr, pa_scr, wc_scr, wa_scr, *, tm, seq, rows):
    d = x_ref.shape[-1]
    hg = HALO // SUBLANES
    gu = hg + tm // SUBLANES
    gc = rows // SUBLANES

    def grouped(v):
        return v.reshape(v.shape[0] // SUBLANES, SUBLANES, d)

    for k in range(CONV_C_WIDTH):
        wc_scr[k] = jnp.broadcast_to(convc_ref[k:k + 1, :], (SUBLANES, d))
    for k in range(CONV_A_WIDTH):
        wa_scr[k] = jnp.broadcast_to(conva_ref[k:k + 1, :], (SUBLANES, d))

    keep = jnp.where((pl.program_id(0) * tm) % seq == 0, 0.0, 1.0)
    ush_scr[0, 0:hg] = grouped(hcv_ref[...].astype(F32) * hcg_ref[...].astype(F32) * keep)
    ush_scr[0, hg:] = grouped(cv_ref[...].astype(F32) * cg_ref[...].astype(F32))
    p_scr[0:hg] = grouped(hac_ref[...].astype(F32) * hau_ref[...].astype(F32) * keep)
    p_scr[hg:] = grouped(ac_ref[...].astype(F32) * au_ref[...].astype(F32))
    u0 = ush_scr[0]
    for r in range(1, SUBLANES):
        ush_scr[r, 0:gu - 1] = _shift_rows(u0, r)

    for c in range(tm // rows):
        g0 = c * gc
        r0 = c * rows
        acc = jnp.broadcast_to(cbias_ref[...][None], (gc, SUBLANES, d))
        for k in range(CONV_C_WIDTH):
            a, r = divmod(HALO - (CONV_C_WIDTH - 1) + k, SUBLANES)
            acc = acc + wc_scr[k][None] * ush_scr[r, g0 + a:g0 + a + gc]
        mu = jnp.mean(acc, axis=-1, keepdims=True)
        xc = acc - mu
        y = xc * lax.rsqrt(jnp.mean(xc * xc, axis=-1, keepdims=True) + EPS) * lng_ref[...][None] + lnb_ref[...][None]
        uc_scr[r0:r0 + rows, :] = (y * _sigmoid(y)).reshape(rows, d).astype(BF16)
        acc = None
        for k in range(CONV_A_WIDTH):
            a, r = divmod(HALO - (CONV_A_WIDTH - 1) + k, SUBLANES)
            if r == 0:
                tap = p_scr[g0 + a:g0 + a + gc]
            else:
                tap = _shift_rows(p_scr[g0 + a:g0 + a + gc + 1], r)
            term = wa_scr[k][None] * tap
            acc = term if acc is None else acc + term
        pa_scr[r0:r0 + rows, :] = (ab_ref[r0:r0 + rows, :].astype(F32) * acc.reshape(rows, d)).astype(BF16)

    merged = ga_ref[...].astype(F32) * _dot(pa_scr[...], woa_ref[...])
    merged = merged + gb_ref[...].astype(F32) * _dot(o_ref[...], wob_ref[...])
    merged = merged + gc_ref[...].astype(F32) * _dot(uc_scr[...], woc_ref[...])
    out_ref[...] = x_ref[...] + _dot(merged.astype(BF16), wo_ref[...])


def _mixer(x, z, o, conv_a, conv_c, conv_c_bias, ln_g, ln_b, w_out_a, w_out_b, w_out_c, w_o,
           layer, cols, *, seq, tm, rows=32):
    n, d = x.shape
    hb = tm // HALO
    cb = {name: c // d for name, c in cols.items()}

    def zspec(name):
        j = cb[name]
        return pl.BlockSpec((tm, d), lambda i: (i, j))

    def hspec(name):
        j = cb[name]
        return pl.BlockSpec((HALO, d), lambda i: (jnp.maximum(i * hb - 1, 0), j))

    def vec(k):
        return pl.BlockSpec((None, k, d), lambda i: (layer, 0, 0))

    wspec = pl.BlockSpec((None, d, d), lambda i: (layer, 0, 0))
    tile = pl.BlockSpec((tm, d), lambda i: (i, 0))
    return pl.pallas_call(
        functools.partial(_mixer_kernel, tm=tm, seq=seq, rows=rows),
        grid=(n // tm,),
        in_specs=[tile] + [zspec(k) for k in ("a_b", "a_c", "a_u", "c_val", "c_gate", "g_a", "g_b", "g_c")]
        + [hspec(k) for k in ("a_c", "a_u", "c_val", "c_gate")] + [tile]
        + [vec(CONV_A_WIDTH), vec(CONV_C_WIDTH), vec(1), vec(1), vec(1)]
        + [wspec] * 4,
        out_specs=tile,
        out_shape=jax.ShapeDtypeStruct((n, d), F32),
        scratch_shapes=[
            pltpu.VMEM((SUBLANES, (tm + HALO) // SUBLANES, SUBLANES, d), F32),
            pltpu.VMEM(((tm + HALO) // SUBLANES, SUBLANES, d), F32),
            pltpu.VMEM((tm, d), BF16),
            pltpu.VMEM((tm, d), BF16),
            pltpu.VMEM((CONV_C_WIDTH, SUBLANES, d), F32),
            pltpu.VMEM((CONV_A_WIDTH, SUBLANES, d), F32),
        ],
        compiler_params=_params("parallel", fuse=[False] * 19 + [True] * 4),
        name="mixer_merge",
    )(x, *([z] * 12), o, conv_a, conv_c, conv_c_bias, ln_g, ln_b, w_out_a, w_out_b, w_out_c, w_o)


def _mem_kv_kernel(mem_ref, g_ref, w_ref, kv_ref):
    kv_ref[...] = _dot(_rms_norm(mem_ref[...], g_ref[...]).astype(BF16), w_ref[...]).astype(kv_ref.dtype)


def _mem_kv(mem, gain, w_xkv):
    m, d = mem.shape
    depth, _, e = w_xkv.shape
    return pl.pallas_call(
        _mem_kv_kernel,
        grid=(depth,),
        in_specs=[
            pl.BlockSpec((m, d), lambda l: (0, 0)),
            pl.BlockSpec((1, d), lambda l: (0, 0)),
            pl.BlockSpec((None, d, e), lambda l: (l, 0, 0)),
        ],
        out_specs=pl.BlockSpec((None, m, e), lambda l: (l, 0, 0)),
        out_shape=jax.ShapeDtypeStruct((depth, m, e), BF16),
        compiler_params=_params("parallel", fuse=[False, False, True]),
        name="mem_kv",
    )(mem, gain, w_xkv)


def _xattn_kernel(x_ref, g_ref, wq_ref, kv_ref, wo_ref, out_ref):
    x = x_ref[...]
    q = _dot(_rms_norm(x, g_ref[...]).astype(BF16), wq_ref[...]) * (X_HEAD_DIM ** -0.5)
    q = q.astype(BF16)
    dx = X_HEADS * X_HEAD_DIM
    outs = []
    for h in range(X_HEADS):
        sl = slice(h * X_HEAD_DIM, (h + 1) * X_HEAD_DIM)
        s = _dot_nt(q[:, sl], kv_ref[:, sl])
        p = jnp.exp(s - jnp.max(s, axis=-1, keepdims=True))
        oh = _dot(p.astype(BF16), kv_ref[:, dx + h * X_HEAD_DIM:dx + (h + 1) * X_HEAD_DIM])
        outs.append((oh / jnp.sum(p, axis=-1, keepdims=True)).astype(BF16))
    out_ref[...] = x + _dot(jnp.concatenate(outs, axis=-1), wo_ref[...])


def _xattn(x, gain, w_xq, kv, w_xo, layer, *, seq, n_mem, tm):
    n, d = x.shape
    dx = w_xq.shape[-1]
    per_seq = seq // tm
    tile = pl.BlockSpec((tm, d), lambda i: (i, 0))
    return pl.pallas_call(
        _xattn_kernel,
        grid=(n // tm,),
        in_specs=[
            tile,
            pl.BlockSpec((None, 1, d), lambda i: (layer, 0, 0)),
            pl.BlockSpec((None, d, dx), lambda i: (layer, 0, 0)),
            pl.BlockSpec((None, None, n_mem, 2 * dx), lambda i: (layer, i // per_seq, 0, 0)),
            pl.BlockSpec((None, dx, d), lambda i: (layer, 0, 0)),
        ],
        out_specs=tile,
        out_shape=jax.ShapeDtypeStruct((n, d), F32),
        compiler_params=_params("parallel", fuse=[False, False, True, False, True]),
        name="xattn",
    )(x, gain, w_xq, kv, w_xo)


def _ffn_kernel(x_ref, g_ref, wg_ref, wu_ref, wd_ref, fg_ref, out_ref, *, final, chunks):
    x = x_ref[...]
    xn = _rms_norm(x, g_ref[...]).astype(BF16)
    y = x
    for c0, c1 in chunks:
        g = _dot(xn, wg_ref[:, c0:c1])
        h = (g * _sigmoid(g) * _dot(xn, wu_ref[:, c0:c1])).astype(BF16)
        y = y + _dot(h, wd_ref[c0:c1, :])
    if final:
        y = _rms_norm(y, fg_ref[...])
    out_ref[...] = y


def _ffn_chunks(dff):
    tiles = dff // MXU_WIDTH
    assert tiles * MXU_WIDTH == dff
    half = (tiles + 1) // 2
    return tuple((a * MXU_WIDTH, b * MXU_WIDTH) for a, b in ((0, half), (half, tiles)) if b > a)


def _ffn(x, gain, w_gate_up, w_down, final_gain, layer, *, tm, final):
    n, d = x.shape
    dff = w_down.shape[1]
    tile = pl.BlockSpec((tm, d), lambda i: (i, 0))
    once = pl.Buffered(1)
    return pl.pallas_call(
        functools.partial(_ffn_kernel, final=final, chunks=_ffn_chunks(dff)),
        grid=(n // tm,),
        in_specs=[
            tile,
            pl.BlockSpec((None, 1, d), lambda i: (layer, 0, 0)),
            pl.BlockSpec((None, d, dff), lambda i: (layer, 0, 0), pipeline_mode=once),
            pl.BlockSpec((None, d, dff), lambda i: (layer, 0, 1), pipeline_mode=once),
            pl.BlockSpec((None, dff, d), lambda i: (layer, 0, 0), pipeline_mode=once),
            pl.BlockSpec((1, d), lambda i: (0, 0)),
        ],
        out_specs=tile,
        out_shape=jax.ShapeDtypeStruct((n, d), F32),
        compiler_params=_params("parallel", fuse=[False, False, True, True, True, False]),
        name="ffn",
    )(x, gain, w_gate_up, w_gate_up, w_down, final_gain)


def _pick(n, pref):
    t = min(n, pref)
    assert n % t == 0, (n, t)
    return t


def kernel(x, mem, mix_norm, w_in, b_gate, b_forget, conv_a, w_out_a, w_out_b, conv_c, conv_c_bias,
           ln_c_gain, ln_c_bias, w_out_c, w_o, xattn_norm, mem_norm, w_xq, w_xkv, w_xo, ffn_norm,
           w_gate_up, w_down, final_norm):
    batch, seq, d = x.shape
    depth = w_in.shape[0]
    n_mem = mem.shape[1]
    n = batch * seq
    d_b = FOX_HEADS * FOX_HEAD_DIM
    assert d % LANES == 0 and seq % CUM_BLOCK == 0 and d_b == d

    names = ("a_b", "a_c", "a_u", "q", "k", "v", "c_val", "c_gate", "g_a", "g_b", "g_c")
    cols = {name: idx * d for idx, name in enumerate(names)}
    f_col = 6 * d
    w_in16 = w_in.astype(BF16)
    w_main = jnp.concatenate([w_in16[:, :, :f_col], w_in16[:, :, f_col + FOX_HEADS:]], axis=-1)
    e_main = w_main.shape[-1]
    w_f = jnp.pad(w_in16[:, :, f_col:f_col + FOX_HEADS], ((0, 0), (0, 0), (0, LANES - FOX_HEADS)))
    b_f = jnp.pad(b_forget, ((0, 0), (0, LANES - FOX_HEADS)))[:, None, :]
    tri = jnp.tril(jnp.ones((CUM_BLOCK, CUM_BLOCK), BF16))
    col_scale = jnp.ones((1, e_main), F32).at[:, cols["q"]:cols["q"] + d_b].set(
        FOX_HEAD_DIM ** -0.5 * LOG2E)

    row3 = lambda a: a[:, None, :]
    bf = lambda a: a.astype(BF16)
    gate_col = cols["c_gate"]
    col_bias = jnp.zeros((depth, 1, e_main), F32).at[:, 0, cols["g_a"]:].set(b_gate)

    kv = _mem_kv(mem.reshape(batch * n_mem, d), mem_norm[None, :], bf(w_xkv))
    kv = kv.reshape(depth, batch, n_mem, kv.shape[-1])
    w_oa, w_ob, w_oc, w_oo = bf(w_out_a), bf(w_out_b), bf(w_out_c), bf(w_o)
    w_q, w_xo_b, w_gu, w_dn = bf(w_xq), bf(w_xo), bf(w_gate_up), bf(w_down)

    xs = x.reshape(n, d)
    for l in range(depth):
        z, f = _in_proj(xs, row3(mix_norm), w_main, w_f, col_scale, col_bias, l, tm=_pick(n, 2048),
                        tn=_pick(d, 1024), gate_col=gate_col)
        c_row = _forget_cumsum(f, b_f, l, tri, batch=batch, seq=seq)
        c_row = c_row.reshape(batch, FOX_HEADS // 2, 2, seq)
        o = _fox_attention(z, c_row, batch=batch, seq=seq, q_col=cols["q"], k_col=cols["k"],
                           v_col=cols["v"], blk=_pick(seq, 512))
        xs = _mixer(xs, z, o, conv_a, conv_c, row3(conv_c_bias), row3(ln_c_gain), row3(ln_c_bias),
                    w_oa, w_ob, w_oc, w_oo, l, cols, seq=seq, tm=_pick(seq, 256))
        xs = _xattn(xs, row3(xattn_norm), w_q, kv, w_xo_b, l, seq=seq, n_mem=n_mem, tm=_pick(seq, 2048))
        xs = _ffn(xs, row3(ffn_norm), w_gu, w_dn, final_norm[None, :], l, tm=_pick(n, 1024),
                  final=(l == depth - 1))
    return xs.reshape(batch, seq, d)
```

```python
import functools

import jax
import jax.numpy as jnp
from jax import lax
from jax.experimental import pallas as pl
from jax.experimental.pallas import tpu as pltpu

EPS = 1e-6
FOX_HEADS = 16
FOX_HEAD_DIM = 64
X_HEADS = 4
X_HEAD_DIM = 128
CONV_A_WIDTH = 3
CONV_C_WIDTH = 31
LANES = 128
HALO = 32
MXU_WIDTH = 256
CUM_BLOCK = MXU_WIDTH
VMEM_LIMIT = 56 * 1024 * 1024
SUBLANES = 8
LOG2E = 1.4426950408889634

F32 = jnp.float32
BF16 = jnp.bfloat16


def _params(*sem, fuse=None):
    return pltpu.CompilerParams(dimension_semantics=sem, vmem_limit_bytes=VMEM_LIMIT, allow_input_fusion=fuse)


def _rms_norm(x, g):
    return x * lax.rsqrt(jnp.mean(x * x, axis=-1, keepdims=True) + EPS) * g


def _sigmoid(x):
    return 0.5 * jnp.tanh(0.5 * x) + 0.5


def _dot(a, b):
    return jnp.dot(a, b, preferred_element_type=F32)


def _dot_nt(a, b):
    return lax.dot_general(a, b, (((1,), (1,)), ((), ())), preferred_element_type=F32)


def _in_proj_kernel(x_ref, g_ref, w_ref, wf_ref, cs_ref, cb_ref, z_ref, f_ref, xn_ref, *, gate_tile):
    j = pl.program_id(1)

    @pl.when(j == 0)
    def _():
        xn = _rms_norm(x_ref[...], g_ref[...]).astype(BF16)
        xn_ref[...] = xn
        f_ref[...] = _dot(xn, wf_ref[...])

    @pl.when(j < gate_tile)
    def _():
        z_ref[...] = (_dot(xn_ref[...], w_ref[...]) * cs_ref[...]).astype(z_ref.dtype)

    @pl.when(j >= gate_tile)
    def _():
        z_ref[...] = _sigmoid(_dot(xn_ref[...], w_ref[...]) + cb_ref[...]).astype(z_ref.dtype)


def _in_proj(x, gain, w_main, w_f, col_scale, col_bias, layer, *, tm, tn, gate_col):
    n, d = x.shape
    e = w_main.shape[-1]
    return pl.pallas_call(
        functools.partial(_in_proj_kernel, gate_tile=gate_col // tn),
        grid=(n // tm, e // tn),
        in_specs=[
            pl.BlockSpec((tm, d), lambda i, j: (i, 0)),
            pl.BlockSpec((None, 1, d), lambda i, j: (layer, 0, 0)),
            pl.BlockSpec((None, d, tn), lambda i, j: (layer, 0, j)),
            pl.BlockSpec((None, d, LANES), lambda i, j: (layer, 0, 0)),
            pl.BlockSpec((1, tn), lambda i, j: (0, j)),
            pl.BlockSpec((None, 1, tn), lambda i, j: (layer, 0, j)),
        ],
        out_specs=[
            pl.BlockSpec((tm, tn), lambda i, j: (i, j)),
            pl.BlockSpec((tm, LANES), lambda i, j: (i, 0)),
        ],
        out_shape=[
            jax.ShapeDtypeStruct((n, e), BF16),
            jax.ShapeDtypeStruct((n, LANES), F32),
        ],
        scratch_shapes=[pltpu.VMEM((tm, d), BF16)],
        compiler_params=_params("parallel", "arbitrary", fuse=[False, False, True, True, False, False]),
        name="in_proj",
    )(x, gain, w_main, w_f, col_scale, col_bias)


def _forget_cumsum(f_ref, b_ref, tri_ref, crow_scr, c_scr):
    x = f_ref[...] + b_ref[...]
    lf = jnp.minimum(x, 0.0) - jnp.log(1.0 + jnp.exp(-jnp.abs(x)))
    hi = lf.astype(BF16)
    rem = lf - hi.astype(F32)
    mid = rem.astype(BF16)
    lo = (rem - mid.astype(F32)).astype(BF16)
    tri = tri_ref[...]
    seq = f_ref.shape[0]
    carry = jnp.zeros((1, LANES), F32)
    for blk in range(seq // CUM_BLOCK):
        sl = slice(blk * CUM_BLOCK, (blk + 1) * CUM_BLOCK)
        part = _dot(tri, hi[sl]) + _dot(tri, mid[sl]) + _dot(tri, lo[sl]) + carry
        c_scr[sl, :] = part
        carry = part[CUM_BLOCK - 1:CUM_BLOCK, :]
    c_t = c_scr[...].T
    for p in range(crow_scr.shape[0]):
        crow_scr[p] = c_t[2 * p:2 * p + 2, :]


def _fox_kernel(q_ref, k_ref, v_ref, f_ref, b_ref, tri_ref, o_ref, crow_ref, c_scr, *, blk):
    lane = lax.broadcasted_iota(jnp.int32, (1, LANES), 1)
    seq = k_ref.shape[0]
    _forget_cumsum(f_ref, b_ref, tri_ref, crow_ref, c_scr)

    def online(state, s, vblk):
        m, acc = state
        m_new = jnp.maximum(m, jnp.max(s, axis=-1, keepdims=True))
        p = jnp.exp2(s - m_new)
        return m_new, jnp.exp2(m - m_new) * acc + _dot(p.astype(BF16), vblk)

    def head_pair(pair, carry):
        lanes = pl.ds(pl.multiple_of(pair * LANES, LANES), LANES)
        for i in range(seq // blk):
            pair_block(pair, lanes, i)
        return carry

    def pair_block(pair, lanes, i):
        q = q_ref[i * blk:(i + 1) * blk, lanes]
        zero = jnp.zeros_like(q)
        qs = (jnp.where(lane < FOX_HEAD_DIM, q, zero), jnp.where(lane < FOX_HEAD_DIM, zero, q))
        init = (jnp.full((blk, 1), -jnp.inf, F32), jnp.zeros((blk, LANES), F32))
        state = [init, init]
        for j in range(i + 1):
            keys = slice(j * blk, (j + 1) * blk)
            v = v_ref[keys, lanes]
            ones = jnp.ones_like(v)
            vs = (jnp.where(lane < FOX_HEAD_DIM, v, ones), jnp.where(lane < FOX_HEAD_DIM, ones, v))
            for h in range(2):
                s = _dot_nt(qs[h], k_ref[keys, lanes]) - crow_ref[pair, h:h + 1, keys] * LOG2E
                if j == i:
                    row = lax.broadcasted_iota(jnp.int32, (blk, blk), 0)
                    col = lax.broadcasted_iota(jnp.int32, (blk, blk), 1)
                    s = jnp.where(row >= col, s, -jnp.inf)
                state[h] = online(state[h], s, vs[h])
        acc0, acc1 = state[0][1], state[1][1]
        num = jnp.where(lane < FOX_HEAD_DIM, acc0, acc1)
        den = jnp.where(lane < FOX_HEAD_DIM, pltpu.roll(acc0, FOX_HEAD_DIM, axis=1), pltpu.roll(acc1, FOX_HEAD_DIM, axis=1))
        o_ref[i * blk:(i + 1) * blk, lanes] = (num / den).astype(o_ref.dtype)

    lax.fori_loop(0, crow_ref.shape[0], head_pair, 0)


def _fox_attention(z, f, b_forget, tri, layer, *, batch, seq, q_col, k_col, v_col, blk):
    n = z.shape[0]
    pairs = FOX_HEADS // 2
    width = pairs * LANES

    def cols(first):
        return pl.BlockSpec((seq, width), lambda b: (b, first // width))

    return pl.pallas_call(
        functools.partial(_fox_kernel, blk=blk),
        grid=(batch,),
        in_specs=[cols(q_col), cols(k_col), cols(v_col),
                  pl.BlockSpec((seq, LANES), lambda b: (b, 0)),
                  pl.BlockSpec((None, 1, LANES), lambda b: (layer, 0, 0)),
                  pl.BlockSpec((CUM_BLOCK, CUM_BLOCK), lambda b: (0, 0))],
        out_specs=cols(0),
        out_shape=jax.ShapeDtypeStruct((n, width), BF16),
        scratch_shapes=[pltpu.VMEM((pairs, 2, seq), F32), pltpu.VMEM((seq, LANES), F32)],
        compiler_params=_params("parallel"),
        name="fox_attention",
    )(z, z, z, f, b_forget, tri)


def _shift_rows(x3, r):
    sub = lax.broadcasted_iota(jnp.int32, x3.shape[1:], 0)[None]
    rolled = pltpu.roll(x3, SUBLANES - r, axis=1)
    return jnp.where(sub < SUBLANES - r, rolled[:-1], rolled[1:])


def _mixer_kernel(x_ref, ab_ref, ac_ref, au_ref, cv_ref, cg_ref, ga_ref, gb_ref, gc_ref,
                  hac_ref, hau_ref, hcv_ref, hcg_ref, o_ref,
                  conva_ref, convc_ref, cbias_ref, lng_ref, lnb_ref,
                  woa_ref, wob_ref, woc_ref, wo_ref, out_ref,
                  ush_scr, p_scr, uc_scr, pa_scr, wc_scr, wa_scr, *, tm, seq, rows):
    d = x_ref.shape[-1]
    hg = HALO // SUBLANES
    gu = hg + tm // SUBLANES
    gc = rows // SUBLANES

    def grouped(v):
        return v.reshape(v.shape[0] // SUBLANES, SUBLANES, d)

    for k in range(CONV_C_WIDTH):
        wc_scr[k] = jnp.broadcast_to(convc_ref[k:k + 1, :], (SUBLANES, d))
    for k in range(CONV_A_WIDTH):
        wa_scr[k] = jnp.broadcast_to(conva_ref[k:k + 1, :], (SUBLANES, d))

    keep = jnp.where((pl.program_id(0) * tm) % seq == 0, 0.0, 1.0)
    ush_scr[0, 0:hg] = grouped(hcv_ref[...].astype(F32) * hcg_ref[...].astype(F32) * keep)
    ush_scr[0, hg:] = grouped(cv_ref[...].astype(F32) * cg_ref[...].astype(F32))
    p_scr[0:hg] = grouped(hac_ref[...].astype(F32) * hau_ref[...].astype(F32) * keep)
    p_scr[hg:] = grouped(ac_ref[...].astype(F32) * au_ref[...].astype(F32))
    u0 = ush_scr[0]
    for r in range(1, SUBLANES):
        ush_scr[r, 0:gu - 1] = _shift_rows(u0, r)

    for c in range(tm // rows):
        g0 = c * gc
        r0 = c * rows
        acc = jnp.broadcast_to(cbias_ref[...][None], (gc, SUBLANES, d))
        for k in range(CONV_C_WIDTH):
            a, r = divmod(HALO - (CONV_C_WIDTH - 1) + k, SUBLANES)
            acc = acc + wc_scr[k][None] * ush_scr[r, g0 + a:g0 + a + gc]
        mu = jnp.mean(acc, axis=-1, keepdims=True)
        xc = acc - mu
        y = xc * lax.rsqrt(jnp.mean(xc * xc, axis=-1, keepdims=True) + EPS) * lng_ref[...][None] + lnb_ref[...][None]
        uc_scr[r0:r0 + rows, :] = (y * _sigmoid(y)).reshape(rows, d).astype(BF16)
        acc = None
        for k in range(CONV_A_WIDTH):
            a, r = divmod(HALO - (CONV_A_WIDTH - 1) + k, SUBLANES)
            if r == 0:
                tap = p_scr[g0 + a:g0 + a + gc]
            else:
                tap = _shift_rows(p_scr[g0 + a:g0 + a + gc + 1], r)
            term = wa_scr[k][None] * tap
            acc = term if acc is None else acc + term
        pa_scr[r0:r0 + rows, :] = (ab_ref[r0:r0 + rows, :].astype(F32) * acc.reshape(rows, d)).astype(BF16)

    merged = ga_ref[...].astype(F32) * _dot(pa_scr[...], woa_ref[...])
    merged = merged + gb_ref[...].astype(F32) * _dot(o_ref[...], wob_ref[...])
    merged = merged + gc_ref[...].astype(F32) * _dot(uc_scr[...], woc_ref[...])
    out_ref[...] = x_ref[...] + _dot(merged.astype(BF16), wo_ref[...])


def _mixer(x, z, o, conv_a, conv_c, conv_c_bias, ln_g, ln_b, w_out_a, w_out_b, w_out_c, w_o,
           layer, cols, *, seq, tm, rows=32):
    n, d = x.shape
    hb = tm // HALO
    cb = {name: c // d for name, c in cols.items()}

    def zspec(name):
        j = cb[name]
        return pl.BlockSpec((tm, d), lambda i: (i, j))

    def hspec(name):
        j = cb[name]
        return pl.BlockSpec((HALO, d), lambda i: (jnp.maximum(i * hb - 1, 0), j))

    def vec(k):
        return pl.BlockSpec((None, k, d), lambda i: (layer, 0, 0))

    wspec = pl.BlockSpec((None, d, d), lambda i: (layer, 0, 0))
    tile = pl.BlockSpec((tm, d), lambda i: (i, 0))
    return pl.pallas_call(
        functools.partial(_mixer_kernel, tm=tm, seq=seq, rows=rows),
        grid=(n // tm,),
        in_specs=[tile] + [zspec(k) for k in ("a_b", "a_c", "a_u", "c_val", "c_gate", "g_a", "g_b", "g_c")]
        + [hspec(k) for k in ("a_c", "a_u", "c_val", "c_gate")] + [tile]
        + [vec(CONV_A_WIDTH), vec(CONV_C_WIDTH), vec(1), vec(1), vec(1)]
        + [wspec] * 4,
        out_specs=tile,
        out_shape=jax.ShapeDtypeStruct((n, d), F32),
        scratch_shapes=[
            pltpu.VMEM((SUBLANES, (tm + HALO) // SUBLANES, SUBLANES, d), F32),
            pltpu.VMEM(((tm + HALO) // SUBLANES, SUBLANES, d), F32),
            pltpu.VMEM((tm, d), BF16),
            pltpu.VMEM((tm, d), BF16),
            pltpu.VMEM((CONV_C_WIDTH, SUBLANES, d), F32),
            pltpu.VMEM((CONV_A_WIDTH, SUBLANES, d), F32),
        ],
        compiler_params=_params("parallel", fuse=[False] * 19 + [True] * 4),
        name="mixer_merge",
    )(x, *([z] * 12), o, conv_a, conv_c, conv_c_bias, ln_g, ln_b, w_out_a, w_out_b, w_out_c, w_o)


def _mem_kv_kernel(mem_ref, g_ref, w_ref, kv_ref):
    kv_ref[...] = _dot(_rms_norm(mem_ref[...], g_ref[...]).astype(BF16), w_ref[...]).astype(kv_ref.dtype)


def _mem_kv(mem, gain, w_xkv):
    m, d = mem.shape
    depth, _, e = w_xkv.shape
    return pl.pallas_call(
        _mem_kv_kernel,
        grid=(depth,),
        in_specs=[
            pl.BlockSpec((m, d), lambda l: (0, 0)),
            pl.BlockSpec((1, d), lambda l: (0, 0)),
            pl.BlockSpec((None, d, e), lambda l: (l, 0, 0)),
        ],
        out_specs=pl.BlockSpec((None, m, e), lambda l: (l, 0, 0)),
        out_shape=jax.ShapeDtypeStruct((depth, m, e), BF16),
        compiler_params=_params("parallel", fuse=[False, False, True]),
        name="mem_kv",
    )(mem, gain, w_xkv)


def _xattn_kernel(x_ref, g_ref, wq_ref, kv_ref, wo_ref, out_ref):
    x = x_ref[...]
    q = _dot(_rms_norm(x, g_ref[...]).astype(BF16), wq_ref[...]) * (X_HEAD_DIM ** -0.5)
    q = q.astype(BF16)
    dx = X_HEADS * X_HEAD_DIM
    outs = []
    for h in range(X_HEADS):
        sl = slice(h * X_HEAD_DIM, (h + 1) * X_HEAD_DIM)
        s = _dot_nt(q[:, sl], kv_ref[:, sl])
        p = jnp.exp(s - jnp.max(s, axis=-1, keepdims=True))
        oh = _dot(p.astype(BF16), kv_ref[:, dx + h * X_HEAD_DIM:dx + (h + 1) * X_HEAD_DIM])
        outs.append((oh / jnp.sum(p, axis=-1, keepdims=True)).astype(BF16))
    out_ref[...] = x + _dot(jnp.concatenate(outs, axis=-1), wo_ref[...])


def _xattn(x, gain, w_xq, kv, w_xo, layer, *, seq, n_mem, tm):
    n, d = x.shape
    dx = w_xq.shape[-1]
    per_seq = seq // tm
    tile = pl.BlockSpec((tm, d), lambda i: (i, 0))
    return pl.pallas_call(
        _xattn_kernel,
        grid=(n // tm,),
        in_specs=[
            tile,
            pl.BlockSpec((None, 1, d), lambda i: (layer, 0, 0)),
            pl.BlockSpec((None, d, dx), lambda i: (layer, 0, 0)),
            pl.BlockSpec((None, None, n_mem, 2 * dx), lambda i: (layer, i // per_seq, 0, 0)),
            pl.BlockSpec((None, dx, d), lambda i: (layer, 0, 0)),
        ],
        out_specs=tile,
        out_shape=jax.ShapeDtypeStruct((n, d), F32),
        compiler_params=_params("parallel", fuse=[False, False, True, False, True]),
        name="xattn",
    )(x, gain, w_xq, kv, w_xo)


def _ffn_kernel(x_ref, g_ref, wg_ref, wu_ref, wd_ref, fg_ref, out_ref, *, final, chunks):
    x = x_ref[...]
    xn = _rms_norm(x, g_ref[...]).astype(BF16)
    y = x
    for c0, c1 in chunks:
        g = _dot(xn, wg_ref[:, c0:c1])
        h = (g * _sigmoid(g) * _dot(xn, wu_ref[:, c0:c1])).astype(BF16)
        y = y + _dot(h, wd_ref[c0:c1, :])
    if final:
        y = _rms_norm(y, fg_ref[...])
    out_ref[...] = y


def _ffn_chunks(dff):
    tiles = dff // MXU_WIDTH
    assert tiles * MXU_WIDTH == dff
    half = (tiles + 1) // 2
    return tuple((a * MXU_WIDTH, b * MXU_WIDTH) for a, b in ((0, half), (half, tiles)) if b > a)


def _ffn(x, gain, w_gate_up, w_down, final_gain, layer, *, tm, final):
    n, d = x.shape
    dff = w_down.shape[1]
    tile = pl.BlockSpec((tm, d), lambda i: (i, 0))
    once = pl.Buffered(1)
    return pl.pallas_call(
        functools.partial(_ffn_kernel, final=final, chunks=_ffn_chunks(dff)),
        grid=(n // tm,),
        in_specs=[
            tile,
            pl.BlockSpec((None, 1, d), lambda i: (layer, 0, 0)),
            pl.BlockSpec((None, d, dff), lambda i: (layer, 0, 0), pipeline_mode=once),
            pl.BlockSpec((None, d, dff), lambda i: (layer, 0, 1), pipeline_mode=once),
            pl.BlockSpec((None, dff, d), lambda i: (layer, 0, 0), pipeline_mode=once),
            pl.BlockSpec((1, d), lambda i: (0, 0)),
        ],
        out_specs=tile,
        out_shape=jax.ShapeDtypeStruct((n, d), F32),
        compiler_params=_params("parallel", fuse=[False, False, True, True, True, False]),
        name="ffn",
    )(x, gain, w_gate_up, w_gate_up, w_down, final_gain)


def _pick(n, pref):
    t = min(n, pref)
    assert n % t == 0, (n, t)
    return t


def kernel(x, mem, mix_norm, w_in, b_gate, b_forget, conv_a, w_out_a, w_out_b, conv_c, conv_c_bias,
           ln_c_gain, ln_c_bias, w_out_c, w_o, xattn_norm, mem_norm, w_xq, w_xkv, w_xo, ffn_norm,
           w_gate_up, w_down, final_norm):
    batch, seq, d = x.shape
    depth = w_in.shape[0]
    n_mem = mem.shape[1]
    n = batch * seq
    d_b = FOX_HEADS * FOX_HEAD_DIM
    assert d % LANES == 0 and seq % CUM_BLOCK == 0 and d_b == d

    names = ("a_b", "a_c", "a_u", "q", "k", "v", "c_val", "c_gate", "g_a", "g_b", "g_c")
    cols = {name: idx * d for idx, name in enumerate(names)}
    f_col = 6 * d
    w_in16 = w_in.astype(BF16)
    w_main = jnp.concatenate([w_in16[:, :, :f_col], w_in16[:, :, f_col + FOX_HEADS:]], axis=-1)
    e_main = w_main.shape[-1]
    w_f = jnp.pad(w_in16[:, :, f_col:f_col + FOX_HEADS], ((0, 0), (0, 0), (0, LANES - FOX_HEADS)))
    b_f = jnp.pad(b_forget, ((0, 0), (0, LANES - FOX_HEADS)))[:, None, :]
    tri = jnp.tril(jnp.ones((CUM_BLOCK, CUM_BLOCK), BF16))
    col_scale = jnp.ones((1, e_main), F32).at[:, cols["q"]:cols["q"] + d_b].set(
        FOX_HEAD_DIM ** -0.5 * LOG2E)

    row3 = lambda a: a[:, None, :]
    bf = lambda a: a.astype(BF16)
    gate_col = cols["c_gate"]
    col_bias = jnp.zeros((depth, 1, e_main), F32).at[:, 0, cols["g_a"]:].set(b_gate)

    kv = _mem_kv(mem.reshape(batch * n_mem, d), mem_norm[None, :], bf(w_xkv))
    kv = kv.reshape(depth, batch, n_mem, kv.shape[-1])
    w_oa, w_ob, w_oc, w_oo = bf(w_out_a), bf(w_out_b), bf(w_out_c), bf(w_o)
    w_q, w_xo_b, w_gu, w_dn = bf(w_xq), bf(w_xo), bf(w_gate_up), bf(w_down)

    xs = x.reshape(n, d)
    for l in range(depth):
        z, f = _in_proj(xs, row3(mix_norm), w_main, w_f, col_scale, col_bias, l, tm=_pick(n, 2048),
                        tn=_pick(d, 1024), gate_col=gate_col)
        o = _fox_attention(z, f, b_f, tri, l, batch=batch, seq=seq, q_col=cols["q"], k_col=cols["k"],
                           v_col=cols["v"], blk=_pick(seq, 512))
        xs = _mixer(xs, z, o, conv_a, conv_c, row3(conv_c_bias), row3(ln_c_gain), row3(ln_c_bias),
                    w_oa, w_ob, w_oc, w_oo, l, cols, seq=seq, tm=_pick(seq, 256))
        xs = _xattn(xs, row3(xattn_norm), w_q, kv, w_xo_b, l, seq=seq, n_mem=n_mem, tm=_pick(seq, 2048))
        xs = _ffn(xs, row3(ffn_norm), w_gu, w_dn, final_norm[None, :], l, tm=_pick(n, 1024),
                  final=(l == depth - 1))
    return xs.reshape(batch, seq, d)
```
